```python
import math
import jax, jax.numpy as jnp
from jax import lax
import numpy as np

D_MODEL = 1024
BATCH = 2
SEQ = 16384
DEPTH = 2

A_HEADS = 4
A_QK_DIM = 64
A_V_DIM = 2 * A_QK_DIM
B_HEADS = 4
B_DIM = 128
CONV_K = 4
CHUNK = 64
C_HEADS = 8
C_HEAD_DIM = 64
C_Q_LORA = 256
C_KV_LORA = 128
IDX_HEADS = 8
IDX_DIM = 64
INDEX_TOPK = 256
N_BRANCH = 3
D_FF = 4 * D_MODEL
Q_BLOCK = 128
DN_ALPHA = (2 * DEPTH) ** 0.25
DN_BETA = (8 * DEPTH) ** -0.25
NORM_EPS = 1e-6

A_QK_COLS = A_HEADS * 2 * A_QK_DIM
A_V_COLS = A_HEADS * A_V_DIM
B_COLS = B_HEADS * B_DIM
C_OUT = C_HEADS * C_HEAD_DIM
SPLIT_SIZES = (A_QK_COLS, A_QK_COLS, A_V_COLS,
               B_COLS, B_COLS, B_COLS, B_COLS, B_HEADS, B_HEADS,
               C_Q_LORA, C_KV_LORA, IDX_DIM, IDX_HEADS,
               N_BRANCH * D_MODEL)
SPLIT_POINTS = tuple(int(v) for v in np.cumsum(SPLIT_SIZES)[:-1])
IN_COLS = int(sum(SPLIT_SIZES))

kernel_name = 'hybrid_diffattn_gdn_dsa_deepnorm'


def layer_norm(x, g, b):
    xf = x.astype(jnp.float32)
    mu = jnp.mean(xf, -1, keepdims=True)
    var = jnp.mean(jnp.square(xf - mu), -1, keepdims=True)
    return ((xf - mu) * lax.rsqrt(var + NORM_EPS) * g + b).astype(x.dtype)


def rms_norm(x, g):
    xf = x.astype(jnp.float32)
    y = xf * lax.rsqrt(jnp.mean(xf * xf, -1, keepdims=True) + NORM_EPS)
    return (y * g).astype(x.dtype)


def l2_norm(x):
    xf = x.astype(jnp.float32)
    return (xf * lax.rsqrt(jnp.sum(xf * xf, -1, keepdims=True) + NORM_EPS)).astype(x.dtype)


def causal_depthwise_conv(x, w):
    k = w.shape[0]
    return lax.conv_general_dilated(x, w[:, None, :].astype(x.dtype), window_strides=(1,),
                                    padding=[(k - 1, 0)], dimension_numbers=('NWC', 'WIO', 'NWC'),
                                    feature_group_count=x.shape[-1])


def to_blocks(z, nb):
    return z.reshape(z.shape[0], nb, Q_BLOCK, *z.shape[2:]).swapaxes(0, 1)


def diff_attention(q, k, v, lam, subln_g, lam_init):
    bsz, t, h = q.shape[:3]
    nb = t // Q_BLOCK
    scale = A_QK_DIM ** -0.5
    key_pos = jnp.arange(t)

    def block(args):
        qi, start = args
        s = jnp.einsum('bqhcd,bshcd->bhcqs', qi, k).astype(jnp.float32) * scale
        qpos = start + jnp.arange(Q_BLOCK)
        s = jnp.where(key_pos[None, :] <= qpos[:, None], s, -jnp.inf)
        p = jax.nn.softmax(s, axis=-1)
        a = p[:, :, 0] - lam * p[:, :, 1]
        return jnp.einsum('bhqs,bshd->bqhd', a.astype(v.dtype), v)

    o = lax.map(block, (to_blocks(q, nb), jnp.arange(nb) * Q_BLOCK))
    o = o.swapaxes(0, 1).reshape(bsz, t, h, v.shape[-1])
    o = rms_norm(o, subln_g) * (1.0 - lam_init)
    return o.reshape(bsz, t, h * v.shape[-1])


def gated_delta_rule(q, k, v, g, beta):
    bsz, t, h, dk = q.shape
    dv = v.shape[-1]
    n = t // CHUNK
    f32 = jnp.float32

    def chunks(z):
        z = z.astype(f32).reshape(bsz, n, CHUNK, h, *z.shape[3:])
        return jnp.moveaxis(z, 3, 1)

    q, k, v, g, beta = (chunks(z) for z in (q, k, v, g, beta))
    q = q * dk ** -0.5
    g = jnp.cumsum(g, axis=-1)
    idx = jnp.arange(CHUNK)
    lower_incl = idx[:, None] >= idx[None, :]
    strict_lower = idx[:, None] > idx[None, :]
    diff = g[..., :, None] - g[..., None, :]
    decay = jnp.where(lower_incl, jnp.exp(jnp.where(lower_incl, diff, 0.0)), 0.0)
    k_beta = k * beta[..., None]
    low = jnp.where(strict_lower, jnp.einsum('bhncd,bhnsd->bhncs', k_beta, k) * decay, 0.0)
    rhs = jnp.concatenate([v * beta[..., None], k_beta * jnp.exp(g)[..., None]], axis=-1)
    sol = lax.linalg.triangular_solve(low + jnp.eye(CHUNK, dtype=f32), rhs,
                                      left_side=True, lower=True, unit_diagonal=True)
    u, w = sol[..., :dv], sol[..., dv:]
    attn_intra = jnp.einsum('bhncd,bhnsd->bhncs', q, k) * decay
    q_dec = q * jnp.exp(g)[..., None]
    k_dec = k * jnp.exp(g[..., -1:] - g)[..., None]
    g_last = jnp.exp(g[..., -1])

    def step(S, xs):
        q_i, k_i, u_i, w_i, a_i, gl_i = xs
        v_new = u_i - jnp.einsum('bhcd,bhde->bhce', w_i, S)
        o = jnp.einsum('bhcd,bhde->bhce', q_i, S) + jnp.einsum('bhcs,bhse->bhce', a_i, v_new)
        S = S * gl_i[..., None, None] + jnp.einsum('bhcd,bhce->bhde', k_i, v_new)
        return S, o

    xs = tuple(jnp.moveaxis(z, 2, 0) for z in (q_dec, k_dec, u, w, attn_intra, g_last))
    _, o = lax.scan(step, jnp.zeros((bsz, h, dk, dv), f32), xs)
    return jnp.transpose(o, (1, 0, 3, 2, 4)).reshape(bsz, t, h, dv)


def gated_deltanet(q, k, v, z, a, b, conv_w, a_log, dt_bias, norm_g):
    bsz, t, _ = q.shape
    qkv = jax.nn.silu(causal_depthwise_conv(jnp.concatenate([q, k, v], axis=-1), conv_w))
    q, k, v = jnp.split(qkv, 3, axis=-1)
    heads = lambda y: y.reshape(bsz, t, B_HEADS, B_DIM)
    q, k, v = l2_norm(heads(q)), l2_norm(heads(k)), heads(v)
    g = -jnp.exp(a_log.astype(jnp.float32)) * jax.nn.softplus(a.astype(jnp.float32) + dt_bias.astype(jnp.float32))
    beta = jax.nn.sigmoid(b.astype(jnp.float32))
    o = gated_delta_rule(q, k, v, g, beta)
    o = rms_norm(o, norm_g) * jax.nn.silu(heads(z).astype(jnp.float32))
    return o.reshape(bsz, t, B_COLS).astype(z.dtype)


def dsa_attention(q_lat, kv_lat, k_idx, w_idx, q_norm_g, kv_norm_g, kidx_g, kidx_b,
                  w_uq, w_qidx, w_uk, w_uv):
    bsz, t, _ = q_lat.shape
    q_lat = rms_norm(q_lat, q_norm_g)
    c_kv = rms_norm(kv_lat, kv_norm_g)
    q = jnp.einsum('btr,rhd->bthd', q_lat, w_uq)
    q_abs = jnp.einsum('bthd,rhd->bthr', q, w_uk)
    q_idx = jnp.einsum('btr,rhd->bthd', q_lat, w_qidx)
    k_idx = layer_norm(k_idx, kidx_g, kidx_b)
    w_idx = w_idx * (IDX_HEADS ** -0.5 * IDX_DIM ** -0.5)
    n_sel = min(INDEX_TOPK, t // 4)
    nb = t // Q_BLOCK
    key_pos = jnp.arange(t)
    scale = C_HEAD_DIM ** -0.5

    def block(args):
        qa, qi, wi, start = args
        qpos = start + jnp.arange(Q_BLOCK)
        s = jax.nn.relu(jnp.einsum('bqhd,bsd->bqhs', qi, k_idx).astype(jnp.float32))
        score = jnp.einsum('bqhs,bqh->bqs', s, wi.astype(jnp.float32))
        score = jnp.where(key_pos[None, :] <= qpos[:, None], score, -jnp.inf)
        _, sel = lax.top_k(score, n_sel)
        valid = sel <= qpos[None, :, None]
        c_sel = jax.vmap(lambda c, i: c[i])(c_kv, sel)
        logits = jnp.einsum('bqhr,bqkr->bqhk', qa, c_sel).astype(jnp.float32) * scale
        logits = jnp.where(valid[:, :, None, :], logits, -jnp.inf)
        p = jax.nn.softmax(logits, axis=-1).astype(c_sel.dtype)
        return jnp.einsum('bqhk,bqkr->bqhr', p, c_sel)

    o_lat = lax.map(block, (to_blocks(q_abs, nb), to_blocks(q_idx, nb), to_blocks(w_idx, nb),
                            jnp.arange(nb) * Q_BLOCK))
    o_lat = o_lat.swapaxes(0, 1).reshape(bsz, t, C_HEADS, C_KV_LORA)
    o = jnp.einsum('bthr,rhd->bthd', o_lat, w_uv)
    return o.reshape(bsz, t, C_OUT)


def setup_inputs(seed: int = 0) -> dict:
    key = jax.random.key(seed)
    ks = iter(jax.random.split(key, 40))
    nrm = lambda shape, s: jax.random.normal(next(ks), shape, jnp.float32) * s
    gain = lambda shape: 1.0 + nrm(shape, 0.02)
    dt = jnp.exp(jax.random.uniform(next(ks), (DEPTH, B_HEADS), jnp.float32,
                                    math.log(1e-3), math.log(1e-1)))
    return {
        'x': nrm((BATCH, SEQ, D_MODEL), 1.0),
        'w_in': nrm((DEPTH, D_MODEL, IN_COLS), D_MODEL ** -0.5),
        'b_gate': nrm((DEPTH, N_BRANCH * D_MODEL), 0.1),
        'a_lambda': nrm((DEPTH, 4, A_QK_DIM), 0.1),
        'a_subln_g': gain((DEPTH, A_V_DIM)),
        'b_conv_w': nrm((DEPTH, CONV_K, 3 * B_COLS), CONV_K ** -0.5),
        'b_a_log': jnp.log(jax.random.uniform(next(ks), (DEPTH, B_HEADS), jnp.float32, 1.0, 16.0)),
        'b_dt_bias': dt + jnp.log(-jnp.expm1(-dt)),
        'b_norm_g': gain((DEPTH, B_DIM)),
        'c_q_norm_g': gain((DEPTH, C_Q_LORA)),
        'c_kv_norm_g': gain((DEPTH, C_KV_LORA)),
        'c_kidx_g': gain((DEPTH, IDX_DIM)),
        'c_kidx_b': nrm((DEPTH, IDX_DIM), 0.02),
        'c_w_uq': nrm((DEPTH, C_Q_LORA, C_HEADS, C_HEAD_DIM), C_Q_LORA ** -0.5),
        'c_w_qidx': nrm((DEPTH, C_Q_LORA, IDX_HEADS, IDX_DIM), C_Q_LORA ** -0.5),
        'c_w_uk': nrm((DEPTH, C_KV_LORA, C_HEADS, C_HEAD_DIM), C_KV_LORA ** -0.5),
        'c_w_uv': nrm((DEPTH, C_KV_LORA, C_HEADS, C_HEAD_DIM), C_KV_LORA ** -0.5),
        'w_branch_a': nrm((DEPTH, A_V_COLS, D_MODEL), A_V_COLS ** -0.5),
        'w_branch_b': nrm((DEPTH, B_COLS, D_MODEL), B_COLS ** -0.5),
        'w_branch_c': nrm((DEPTH, C_OUT, D_MODEL), C_OUT ** -0.5),
        'w_o': nrm((DEPTH, D_MODEL, D_MODEL), D_MODEL ** -0.5 * DN_BETA),
        'ln1_g': gain((DEPTH, D_MODEL)),
        'ln1_b': nrm((DEPTH, D_MODEL), 0.02),
        'w_ff1': nrm((DEPTH, D_MODEL, D_FF), D_MODEL ** -0.5),
        'w_ff2': nrm((DEPTH, D_FF, D_MODEL), D_FF ** -0.5 * DN_BETA),
        'ln2_g': gain((DEPTH, D_MODEL)),
        'ln2_b': nrm((DEPTH, D_MODEL), 0.02),
    }


def reference(x, w_in, b_gate, a_lambda, a_subln_g, b_conv_w, b_a_log, b_dt_bias, b_norm_g,
              c_q_norm_g, c_kv_norm_g, c_kidx_g, c_kidx_b, c_w_uq, c_w_qidx, c_w_uk, c_w_uv,
              w_branch_a, w_branch_b, w_branch_c, w_o, ln1_g, ln1_b, w_ff1, w_ff2, ln2_g, ln2_b):
    bsz, t, _ = x.shape
    for l in range(DEPTH):
        lam_init = 0.8 - 0.6 * math.exp(-0.3 * l)
        lam_p = a_lambda[l].astype(jnp.float32)
        lam = jnp.exp(jnp.sum(lam_p[0] * lam_p[1])) - jnp.exp(jnp.sum(lam_p[2] * lam_p[3])) + lam_init

        proj = x @ w_in[l]
        (aq, ak, av, bq, bk, bv, bz, ba, bb,
         cq, ckv, cki, cwi, gates) = jnp.split(proj, SPLIT_POINTS, axis=-1)

        y_a = diff_attention(aq.reshape(bsz, t, A_HEADS, 2, A_QK_DIM),
                             ak.reshape(bsz, t, A_HEADS, 2, A_QK_DIM),
                             av.reshape(bsz, t, A_HEADS, A_V_DIM),
                             lam, a_subln_g[l], lam_init)
        y_b = gated_deltanet(bq, bk, bv, bz, ba, bb, b_conv_w[l], b_a_log[l], b_dt_bias[l], b_norm_g[l])
        y_c = dsa_attention(cq, ckv, cki, cwi, c_q_norm_g[l], c_kv_norm_g[l], c_kidx_g[l], c_kidx_b[l],
                            c_w_uq[l], c_w_uqidx[l] if False else c_w_qidx[l], c_w_uk[l], c_w_uv[l])

        g = jax.nn.sigmoid(gates + b_gate[l]).reshape(bsz, t, N_BRANCH, D_MODEL)
        merged = (g[:, :, 0] * (y_a @ w_branch_a[l])
                  + g[:, :, 1] * (y_b @ w_branch_b[l])
                  + g[:, :, 2] * (y_c @ w_branch_c[l]))
        x = layer_norm(DN_ALPHA * x + merged @ w_o[l], ln1_g[l], ln1_b[l])

        hdn = jnp.square(jax.nn.relu(x @ w_ff1[l]))
        x = layer_norm(DN_ALPHA * x + hdn @ w_ff2[l], ln2_g[l], ln2_b[l])
    return x
```

```python
import functools
import math

import jax
import jax.numpy as jnp
import numpy as np
from jax import lax
from jax.experimental import pallas as pl
from jax.experimental.pallas import tpu as pltpu

F32 = jnp.float32
BF16 = jnp.bfloat16
I32 = jnp.int32

LANES = 128
NEG = -1e30
INT_MIN = -2 ** 31
INT_MAX = 2 ** 31 - 1
VMEM_LIMIT = 56 * 1024 * 1024

A_HEADS = 4
A_QK_DIM = 64
B_HEADS = 4
B_DIM = 128
CONV_K = 4
GDN_CHUNK = 128
C_HEADS = 8
C_HEAD_DIM = 64
C_Q_LORA = 256
C_KV_LORA = 128
IDX_HEADS = 8
IDX_DIM = 64
INDEX_TOPK = 256
NORM_EPS = 1e-6

OFF_AQ, OFF_AK, OFF_AV = 0, 512, 1024
OFF_BQKV, OFF_BZ = 1536, 3072
OFF_CQ, OFF_CKV, OFF_MISC, OFF_GATES = 3584, 3840, 3968, 4096
P_COLS = 7168
MISC_WI, MISC_A, MISC_B = 64, 72, 76

_NT = (((1,), (1,)), ((), ()))
_NN = (((1,), (0,)), ((), ()))


def _cparams(sem):
    return pltpu.CompilerParams(dimension_semantics=sem, vmem_limit_bytes=VMEM_LIMIT)


def _dot(a, b, dims=_NN):
    return lax.dot_general(a, b, dims, preferred_element_type=F32)


def _split2(a):
    hi = a.astype(BF16)
    lo = (a - hi.astype(F32)).astype(BF16)
    return hi, lo


def _dot3(a, b, dims=_NN):
    ah, al = _split2(a)
    bh, bl = _split2(b)
    return _dot(ah, bh, dims) + (_dot(ah, bl, dims) + _dot(al, bh, dims))


def _layer_norm(x, g, b):
    mu = jnp.mean(x, axis=-1, keepdims=True)
    xc = x - mu
    var = jnp.mean(xc * xc, axis=-1, keepdims=True)
    return xc * lax.rsqrt(var + NORM_EPS) * g + b


def _rms(x):
    return x * lax.rsqrt(jnp.mean(x * x, axis=-1, keepdims=True) + NORM_EPS)


def _sigmoid(x):
    return 1.0 / (1.0 + jnp.exp(-x))


def _mm_kernel(x_ref, w_ref, o_ref):
    o_ref[...] = _dot(x_ref[...].astype(BF16), w_ref[...]).astype(o_ref.dtype)


def _matmul(x, w, out_dtype, tm, tn):
    m, k = x.shape
    n = w.shape[1]
    return pl.pallas_call(
        _mm_kernel,
        out_shape=jax.ShapeDtypeStruct((m, n), out_dtype),
        grid=(m // tm, n // tn),
        in_specs=[pl.BlockSpec((tm, k), lambda i, j: (i, 0)),
                  pl.BlockSpec((k, tn), lambda i, j: (0, j))],
        out_specs=pl.BlockSpec((tm, tn), lambda i, j: (i, j)),
        compiler_params=_cparams(("parallel", "parallel")),
        name="in_proj",
    )(x, w)


def _pack_w_in(w):
    d = w.shape[0]
    misc = jnp.concatenate(
        [w[:, 3976:4040], w[:, 4040:4048], w[:, 3584:3588], w[:, 3588:3592],
         jnp.zeros((d, LANES - 80), w.dtype)], axis=1)
    packed = jnp.concatenate(
        [w[:, 0:3584], w[:, 3592:3848], w[:, 3848:3976], misc, w[:, 4048:7120]], axis=1)
    return packed.astype(BF16)


def _diffattn_kernel(it_ref, jt_ref, lam_ref, g_ref, q_ref, k_ref, v_ref, o_ref,
                     m_scr, l_scr, acc_scr, *, tq, lam_init):
    s_idx = pl.program_id(2)
    i = it_ref[s_idx]
    j = jt_ref[s_idx]

    @pl.when(j == 0)
    def _():
        m_scr[...] = jnp.full(m_scr.shape, NEG, F32)
        l_scr[...] = jnp.zeros(l_scr.shape, F32)
        acc_scr[...] = jnp.zeros(acc_scr.shape, F32)

    def step(diagonal):
        q = q_ref[...] * (A_QK_DIM ** -0.5)
        lane = lax.broadcasted_iota(I32, q.shape, 1)
        kb = k_ref[...].astype(BF16)
        vb = v_ref[...].astype(BF16)
        for c in range(2):
            keep = (lane < A_QK_DIM) if c == 0 else (lane >= A_QK_DIM)
            qc = jnp.where(keep, q, 0.0).astype(BF16)
            s = _dot(qc, kb, _NT)
            if diagonal:
                r = lax.broadcasted_iota(I32, s.shape, 0)
                cc = lax.broadcasted_iota(I32, s.shape, 1)
                s = jnp.where(cc <= r, s, NEG)
            m_prev = m_scr[c]
            m_new = jnp.maximum(m_prev, jnp.max(s, axis=-1, keepdims=True))
            alpha = jnp.exp(m_prev - m_new)
            p = jnp.exp(s - jnp.tile(m_new, (1, tq // LANES)))
            l_scr[c] = alpha * l_scr[c] + jnp.sum(p, axis=-1, keepdims=True)
            m_scr[c] = m_new
            acc_scr[c] = alpha * acc_scr[c] + _dot(p.astype(BF16), vb)

    @pl.when(j < i)
    def _():
        step(False)

    @pl.when(j == i)
    def _():
        step(True)
        lp = lam_ref[...]
        s1 = jnp.sum(lp[0:1, :] * lp[1:2, :], axis=-1, keepdims=True)
        s2 = jnp.sum(lp[2:3, :] * lp[3:4, :], axis=-1, keepdims=True)
        lam = jnp.exp(s1) - jnp.exp(s2) + lam_init
        o = acc_scr[0] / l_scr[0] - lam * (acc_scr[1] / l_scr[1])
        o_ref[...] = _rms(o) * g_ref[...] * (1.0 - lam_init)


def _diff_attention(proj, a_lambda, subln_g, bsz, t, lam_init, tq=512):
    tq = min(tq, t)
    nq = t // tq
    it, jt = np.tril_indices(nq)
    it = jnp.asarray(it, I32)
    jt = jnp.asarray(jt, I32)
    nsteps = int(it.shape[0])
    qo, ko, vo = OFF_AQ // LANES, OFF_AK // LANES, OFF_AV // LANES
    grid_spec = pltpu.PrefetchScalarGridSpec(
        num_scalar_prefetch=2,
        grid=(bsz, A_HEADS, nsteps),
        in_specs=[
            pl.BlockSpec((4, A_QK_DIM), lambda b, h, s, it, jt: (0, 0)),
            pl.BlockSpec((1, LANES), lambda b, h, s, it, jt: (0, 0)),
            pl.BlockSpec((tq, LANES), lambda b, h, s, it, jt: (b * nq + it[s], qo + h)),
            pl.BlockSpec((tq, LANES), lambda b, h, s, it, jt: (b * nq + jt[s], ko + h)),
            pl.BlockSpec((tq, LANES), lambda b, h, s, it, jt: (b * nq + jt[s], vo + h)),
        ],
        out_specs=pl.BlockSpec((tq, LANES), lambda b, h, s, it, jt: (b * nq + it[s], h)),
        scratch_shapes=[pltpu.VMEM((2, tq, LANES), F32)] * 3,
    )
    return pl.pallas_call(
        functools.partial(_diffattn_kernel, tq=tq, lam_init=lam_init),
        grid_spec=grid_spec,
        out_shape=jax.ShapeDtypeStruct((bsz * t, A_HEADS * LANES), F32),
        compiler_params=_cparams(("parallel", "parallel", "arbitrary")),
        name="diff_attention",
    )(it, jt, a_lambda, subln_g.reshape(1, LANES), proj, proj, proj)


def _gdn_prep_kernel(qkv_ref, misc_ref, cw_ref, gp_ref,
                     u_ref, w_ref, qd_ref, kdt_ref, attn_ref, gl_ref, xs_scr):
    c_rows = GDN_CHUNK
    n = pl.program_id(1)

    @pl.when(n == 0)
    def _():
        xs_scr[0:8, :] = jnp.zeros((8, xs_scr.shape[1]), F32)

    xs_scr[8:8 + c_rows, :] = qkv_ref[...]
    cw = cw_ref[...]
    y = cw[0:1, :] * xs_scr[5:5 + c_rows, :]
    for jj in range(1, CONV_K):
        y = y + cw[jj:jj + 1, :] * xs_scr[5 + jj:5 + jj + c_rows, :]
    xs_scr[0:8, :] = xs_scr[c_rows:c_rows + 8, :]
    y = y * _sigmoid(y)

    mi = misc_ref[...]
    gp = gp_ref[...]
    xg = mi + gp[1:2, :]
    softplus = jnp.maximum(xg, 0.0) + jnp.log(1.0 + jnp.exp(-jnp.abs(xg)))
    g_all = -jnp.exp(gp[0:1, :]) * softplus
    beta_all = _sigmoid(mi)

    r = lax.broadcasted_iota(I32, (c_rows, c_rows), 0)
    c = lax.broadcasted_iota(I32, (c_rows, c_rows), 1)
    incl = r >= c
    strict = r > c
    eye = (r == c).astype(F32)
    ones_l = incl.astype(BF16)
    g_hi = g_all.astype(BF16)
    g_r1 = g_all - g_hi.astype(F32)
    g_mid = g_r1.astype(BF16)
    g_lo = (g_r1 - g_mid.astype(F32)).astype(BF16)
    gc_all = _dot(ones_l, g_hi) + (_dot(ones_l, g_mid) + _dot(ones_l, g_lo))
    gc_t = gc_all.T

    for h in range(B_HEADS):
        sl = slice(h * B_DIM, (h + 1) * B_DIM)
        qh = y[:, h * B_DIM:(h + 1) * B_DIM]
        kh = y[:, 512 + h * B_DIM:512 + (h + 1) * B_DIM]
        vh = y[:, 1024 + h * B_DIM:1024 + (h + 1) * B_DIM]
        qh = qh * lax.rsqrt(jnp.sum(qh * qh, axis=-1, keepdims=True) + NORM_EPS) * (B_DIM ** -0.5)
        kh = kh * lax.rsqrt(jnp.sum(kh * kh, axis=-1, keepdims=True) + NORM_EPS)
        gcol = gc_all[:, MISC_A + h:MISC_A + h + 1]
        grow = gc_t[MISC_A + h:MISC_A + h + 1, :]
        bcol = beta_all[:, MISC_B + h:MISC_B + h + 1]
        decay = jnp.where(incl, jnp.exp(jnp.where(incl, gcol - grow, 0.0)), 0.0)
        kb = kh * bcol
        khb = kh.astype(BF16)
        low = jnp.where(strict, _dot(kb.astype(BF16), khb, _NT) * decay, 0.0)

        low16 = jnp.where((r >> 4) == (c >> 4), low, 0.0)
        pw = low16
        inv = eye - low16
        for _ in range(3):
            pw = _dot3(pw, pw)
            inv = inv + _dot3(inv, pw)
        for sh in (4, 5, 6):
            off = jnp.where((r >> (sh + 1)) == (c >> (sh + 1)),
                            jnp.where((r >> sh) != (c >> sh), low, 0.0), 0.0)
            inv = inv - _dot3(inv, _dot3(off, inv))

        eg = jnp.exp(gcol)
        rhs = jnp.concatenate([vh * bcol, kb * eg], axis=1)
        sol = _dot3(inv, rhs)
        glast = gc_all[c_rows - 1:c_rows, MISC_A + h:MISC_A + h + 1]
        u_ref[:, sl] = sol[:, :B_DIM]
        w_ref[:, sl] = sol[:, B_DIM:]
        qd_ref[:, sl] = qh * eg
        kdt_ref[:, sl] = (kh * jnp.exp(glast - gcol)).T
        attn_ref[:, sl] = _dot(qh.astype(BF16), khb, _NT) * decay
        gl_ref[:, sl] = jnp.broadcast_to(jnp.exp(glast), (8, B_DIM))


def _gdn_scan_kernel(u_ref, w_ref, qd_ref, kdt_ref, attn_ref, gl_ref, z_ref, g_ref, o_ref, s_scr):
    n = pl.program_id(1)

    @pl.when(n == 0)
    def _():
        s_scr[...] = jnp.zeros(s_scr.shape, F32)

    g = g_ref[...]
    for h in range(B_HEADS):
        sl = slice(h * B_DIM, (h + 1) * B_DIM)
        s = s_scr[h]
        sb = s.astype(BF16)
        v_new = u_ref[:, sl] - _dot(w_ref[:, sl].astype(BF16), sb)
        vb = v_new.astype(BF16)
        o = _dot(qd_ref[:, sl].astype(BF16), sb) + _dot(attn_ref[:, sl].astype(BF16), vb)
        s_scr[h] = s * gl_ref[0:1, sl] + _dot(kdt_ref[:, sl].astype(BF16), vb)
        z = z_ref[:, sl]
        o_ref[:, sl] = _rms(o) * g * (z * _sigmoid(z))


def _gated_deltanet(proj, conv_w, a_log, dt_bias, norm_g, bsz, t):
    m = bsz * t
    c_rows = GDN_CHUNK
    n = t // c_rows
    width = B_HEADS * B_DIM
    gp = jnp.zeros((8, LANES), F32)
    gp = gp.at[0, MISC_A:MISC_A + B_HEADS].set(a_log.astype(F32))
    gp = gp.at[1, MISC_A:MISC_A + B_HEADS].set(dt_bias.astype(F32))
    row = lambda b, i: (b * n + i, 0)
    big = pl.BlockSpec((c_rows, width), row)
    small = pl.BlockSpec((8, width), row)
    u, w, qd, kdt, attn, gl = pl.pallas_call(
        _gdn_prep_kernel,
        out_shape=[jax.ShapeDtypeStruct((m, width), F32)] * 5
        + [jax.ShapeDtypeStruct((bsz * n * 8, width), F32)],
        grid=(bsz, n),
        in_specs=[pl.BlockSpec((c_rows, 3 * width), lambda b, i: (b * n + i, OFF_BQKV // (3 * width))),
                  pl.BlockSpec((c_rows, LANES), lambda b, i: (b * n + i, OFF_MISC // LANES)),
                  pl.BlockSpec((CONV_K, 3 * width), lambda b, i: (0, 0)),
                  pl.BlockSpec((8, LANES), lambda b, i: (0, 0))],
        out_specs=[big] * 5 + [small],
        scratch_shapes=[pltpu.VMEM((c_rows + 8, 3 * width), F32)],
        compiler_params=_cparams(("parallel", "arbitrary")),
        name="gdn_prep",
    )(proj, proj, conv_w, gp)
    return pl.pallas_call(
        _gdn_scan_kernel,
        out_shape=jax.ShapeDtypeStruct((m, width), F32),
        grid=(bsz, n),
        in_specs=[big] * 5 + [small,
                              pl.BlockSpec((c_rows, width), lambda b, i: (b * n + i, OFF_BZ // width)),
                              pl.BlockSpec((1, B_DIM), lambda b, i: (0, 0))],
        out_specs=big,
        scratch_shapes=[pltpu.VMEM((B_HEADS, B_DIM, B_DIM), F32)],
        compiler_params=_cparams(("parallel", "arbitrary")),
        name="gdn_scan",
    )(u, w, qd, kdt, attn, gl, proj, norm_g.reshape(1, B_DIM))


def _dsa_prep_kernel(cq_ref, ckv_ref, misc_ref, qg_ref, kvg_ref, kig_ref, kib_ref,
                     wuq_ref, wuk_ref, wqi_ref, qa_ref, qi_ref, ckvn_ref, kpad_ref, wi_ref):
    ql = (_rms(cq_ref[...]) * qg_ref[...]).astype(BF16)
    qh = _dot(ql, wuq_ref[...]).astype(BF16)
    qa_ref[...] = (_dot(qh, wuk_ref[...]) * (C_HEAD_DIM ** -0.5)).astype(BF16)
    qi_ref[...] = _dot(ql, wqi_ref[...]).astype(BF16)
    ckvn_ref[...] = (_rms(ckv_ref[...]) * kvg_ref[...]).astype(BF16)
    mi = misc_ref[...]
    lane = lax.broadcasted_iota(I32, mi.shape, 1)
    isk = lane < IDX_DIM
    mu = jnp.sum(jnp.where(isk, mi, 0.0), axis=-1, keepdims=True) * (1.0 / IDX_DIM)
    xc = jnp.where(isk, mi - mu, 0.0)
    var = jnp.sum(xc * xc, axis=-1, keepdims=True) * (1.0 / IDX_DIM)
    kn = xc * lax.rsqrt(var + NORM_EPS) * kig_ref[...] + kib_ref[...]
    kpad_ref[...] = jnp.where(isk, kn, 0.0).astype(BF16)
    wi_ref[...] = mi * (IDX_HEADS ** -0.5 * IDX_DIM ** -0.5)


def _float_key(x):
    b = lax.bitcast_convert_type(x, I32)
    return jnp.where(b >= 0, b, b ^ INT_MAX)


def _dsa_main_kernel(qa_ref, qi_ref, wi_ref, kpad_ref, ckv_ref, wuv_ref, o_ref,
                     key_scr, m_scr, l_scr, acc_scr, *, tq, tk, n_sel, idx_bits):
    i = pl.program_id(1)
    nvis = ((i + 1) * tq + tk - 1) // tk
    rows = i * tq + lax.broadcasted_iota(I32, (tq, tk), 0)
    col0 = lax.broadcasted_iota(I32, (tq, tk), 1)
    wi = wi_ref[...]

    def score_block(jb, carry):
        start = pl.multiple_of(jb * tk, tk)
        kblk = kpad_ref[pl.ds(start, tk), :]
        acc = None
        for h in range(IDX_HEADS):
            s = _dot(qi_ref[:, h * LANES:(h + 1) * LANES], kblk, _NT)
            term = jnp.maximum(s, 0.0) * wi[:, MISC_WI + h:MISC_WI + h + 1]
            acc = term if acc is None else acc + term
        key_scr[jb] = jnp.where(col0 + start <= rows, _float_key(acc), INT_MIN)
        return carry

    lax.fori_loop(0, nvis, score_block, 0)

    def count(hit_fn):
        def body(jb, acc):
            hit = hit_fn(key_scr[jb], col0 + jb * tk)
            part = hit[:, 0:LANES]
            for cc in range(1, tk // LANES):
                part = part + hit[:, cc * LANES:(cc + 1) * LANES]
            return acc + part
        acc = lax.fori_loop(0, nvis, body, jnp.zeros((tq, LANES), I32))
        return jnp.sum(acc.astype(F32), axis=-1, keepdims=True)

    kf = float(n_sel)

    def radix_step(it, carry):
        ans, cnt_ans = carry
        cand = ans | lax.shift_left(jnp.int32(1), 31 - it)
        cand_s = cand ^ INT_MIN
        cnt = count(lambda kblk, col: jnp.where(kblk >= cand_s, 1, 0))
        ok = cnt >= kf
        return jnp.where(ok, cand, ans), jnp.where(ok, cnt, cnt_ans)

    ans, cnt_ge = lax.fori_loop(
        0, 32, radix_step, (jnp.zeros((tq, 1), I32), jnp.full((tq, 1), kf, F32)))
    thr = ans ^ INT_MIN

    qpos = i * tq + lax.broadcasted_iota(I32, (tq, 1), 0)
    few = qpos < n_sel
    thr = jnp.where(few, INT_MIN, thr)
    overflow = jnp.where(few, 0.0, cnt_ge - kf)
    cut_init = jnp.where(few, -1, INT_MAX)

    def resolve_ties():
        cnt_gt = count(lambda kblk, col: jnp.where(kblk > thr, 1, 0))
        need = kf - cnt_gt

        def cut_step(it, cut):
            cand = cut | lax.shift_left(jnp.int32(1), idx_bits - 1 - it)
            below = count(lambda kblk, col: jnp.where(kblk == thr, jnp.where(col < cand, 1, 0), 0))
            return jnp.where(below < need, cand, cut)

        cut = lax.fori_loop(0, idx_bits, cut_step, jnp.zeros((tq, 1), I32))
        return jnp.where(overflow > 0.0, cut, cut_init)

    cut = lax.cond(jnp.max(overflow) > 0.0, resolve_ties, lambda: cut_init)

    m_scr[...] = jnp.full(m_scr.shape, NEG, F32)
    l_scr[...] = jnp.zeros(l_scr.shape, F32)
    acc_scr[...] = jnp.zeros(acc_scr.shape, F32)

    def attend_block(jb, carry):
        start = pl.multiple_of(jb * tk, tk)
        kblk = key_scr[jb]
        col = col0 + start
        bias = jnp.where(kblk > thr, 0.0, jnp.where(kblk == thr, jnp.where(col <= cut, 0.0, NEG), NEG))
        ckv = ckv_ref[pl.ds(start, tk), :]
        for h in range(C_HEADS):
            s = _dot(qa_ref[:, h * LANES:(h + 1) * LANES], ckv, _NT) + bias
            m_prev = m_scr[h]
            m_new = jnp.maximum(m_prev, jnp.max(s, axis=-1, keepdims=True))
            alpha = jnp.exp(m_prev - m_new)
            p = jnp.exp(s - jnp.tile(m_new, (1, tk // LANES)))
            l_scr[h] = alpha * l_scr[h] + jnp.sum(p, axis=-1, keepdims=True)
            m_scr[h] = m_new
            acc_scr[h] = alpha * acc_scr[h] + _dot(p.astype(BF16), ckv)
        return carry

    lax.fori_loop(0, nvis, attend_block, 0)
    o_lat = jnp.concatenate([acc_scr[h] / l_scr[h] for h in range(C_HEADS)], axis=1)
    o_ref[...] = _dot(o_lat.astype(BF16), wuv_ref[...])


def _dsa_attention(proj, q_norm_g, kv_norm_g, kidx_g, kidx_b, w_uq, w_qidx, w_uk, w_uv,
                   bsz, t, tm=512, tq=256, tk=512):
    m = bsz * t
    tm = min(tm, t)
    tq = min(tq, t)
    tk = min(tk, t)
    hr = C_HEADS * C_KV_LORA
    wuq = w_uq.reshape(C_Q_LORA, C_HEADS * C_HEAD_DIM).astype(BF16)
    eye_h = jnp.eye(C_HEADS, dtype=F32)
    wuk_bd = jnp.einsum('rhd,hg->hdgr', w_uk, eye_h).reshape(C_HEADS * C_HEAD_DIM, hr).astype(BF16)
    wuv_bd = jnp.einsum('rhd,hg->hrgd', w_uv, eye_h).reshape(hr, C_HEADS * C_HEAD_DIM).astype(BF16)
    wqi = jnp.pad(w_qidx, ((0, 0), (0, 0), (0, LANES - IDX_DIM))).reshape(C_Q_LORA, IDX_HEADS * LANES).astype(BF16)
    pad_row = lambda v: jnp.pad(v.astype(F32), (0, LANES - IDX_DIM)).reshape(1, LANES)
    full = lambda shape: pl.BlockSpec(shape, lambda i: (0, 0))
    qa, qi, ckvn, kpad, wi = pl.pallas_call(
        _dsa_prep_kernel,
        out_shape=[jax.ShapeDtypeStruct((m, hr), BF16), jax.ShapeDtypeStruct((m, IDX_HEADS * LANES), BF16),
                   jax.ShapeDtypeStruct((m, LANES), BF16), jax.ShapeDtypeStruct((m, LANES), BF16),
                   jax.ShapeDtypeStruct((m, LANES), F32)],
        grid=(m // tm,),
        in_specs=[pl.BlockSpec((tm, C_Q_LORA), lambda i: (i, OFF_CQ // C_Q_LORA)),
                  pl.BlockSpec((tm, LANES), lambda i: (i, OFF_CKV // LANES)),
                  pl.BlockSpec((tm, LANES), lambda i: (i, OFF_MISC // LANES)),
                  full((1, C_Q_LORA)), full((1, LANES)), full((1, LANES)), full((1, LANES)),
                  full(wuq.shape), full(wuk_bd.shape), full(wqi.shape)],
        out_specs=[pl.BlockSpec((tm, hr), lambda i: (i, 0)), pl.BlockSpec((tm, IDX_HEADS * LANES), lambda i: (i, 0)),
                   pl.BlockSpec((tm, LANES), lambda i: (i, 0)), pl.BlockSpec((tm, LANES), lambda i: (i, 0)),
                   pl.BlockSpec((tm, LANES), lambda i: (i, 0))],
        compiler_params=_cparams(("parallel",)),
        name="dsa_prep",
    )(proj, proj, proj, q_norm_g.reshape(1, C_Q_LORA), kv_norm_g.reshape(1, LANES),
      pad_row(kidx_g), pad_row(kidx_b), wuq, wuk_bd, wqi)

    nq = t // tq
    n_sel = min(INDEX_TOPK, t // 4)
    idx_bits = max(1, int(math.ceil(math.log2(t))))
    blk = lambda w: pl.BlockSpec((tq, w), lambda b, i: (b * nq + i, 0))
    seq = lambda w: pl.BlockSpec((t, w), lambda b, i: (b, 0))
    return pl.pallas_call(
        functools.partial(_dsa_main_kernel, tq=tq, tk=tk, n_sel=n_sel, idx_bits=idx_bits),
        out_shape=jax.ShapeDtypeStruct((m, C_HEADS * C_HEAD_DIM), F32),
        grid=(bsz, nq),
        in_specs=[blk(hr), blk(IDX_HEADS * LANES), blk(LANES), seq(LANES), seq(LANES),
                  pl.BlockSpec(wuv_bd.shape, lambda b, i: (0, 0))],
        out_specs=blk(C_HEADS * C_HEAD_DIM),
        scratch_shapes=[pltpu.VMEM((t // tk, tq, tk), I32)] + [pltpu.VMEM((C_HEADS, tq, LANES), F32)] * 3,
        compiler_params=_cparams(("parallel", "arbitrary")),
        name="dsa_main",
    )(qa, qi, wi, kpad, ckvn, wuv_bd)


def _merge_kernel(x_ref, g0_ref, g1_ref, g2_ref, ya_ref, yb_ref, yc_ref, bg_ref,
                  wa_ref, wb_ref, wc_ref, wo_ref, lg_ref, lb_ref, o_ref, *, alpha):
    bg = bg_ref[...]
    merged = None
    for idx, (g_ref, y_ref, w_ref) in enumerate(
            ((g0_ref, ya_ref, wa_ref), (g1_ref, yb_ref, wb_ref), (g2_ref, yc_ref, wc_ref))):
        gate = _sigmoid(g_ref[...] + bg[idx:idx + 1, :])
        term = gate * _dot(y_ref[...].astype(BF16), w_ref[...])
        merged = term if merged is None else merged + term
    y = alpha * x_ref[...] + _dot(merged.astype(BF16), wo_ref[...])
    o_ref[...] = _layer_norm(y, lg_ref[...], lb_ref[...])


def _merge(x, proj, ya, yb, yc, b_gate, wa, wb, wc, wo, ln_g, ln_b, alpha, tm=256):
    m, d = x.shape
    gate_blk = lambda k: pl.BlockSpec((tm, d), lambda i: (i, OFF_GATES // d + k))
    rowblk = lambda w: pl.BlockSpec((tm, w), lambda i: (i, 0))
    full = lambda a: pl.BlockSpec(a.shape, lambda i: (0, 0))
    bg = b_gate.reshape(3, d)
    ws = [wa.astype(BF16), wb.astype(BF16), wc.astype(BF16), wo.astype(BF16)]
    lg, lb = ln_g.reshape(1, d), ln_b.reshape(1, d)
    return pl.pallas_call(
        functools.partial(_merge_kernel, alpha=alpha),
        out_shape=jax.ShapeDtypeStruct((m, d), F32),
        grid=(m // tm,),
        in_specs=[rowblk(d), gate_blk(0), gate_blk(1), gate_blk(2),
                  rowblk(ya.shape[1]), rowblk(yb.shape[1]), rowblk(yc.shape[1]), full(bg)]
        + [full(w) for w in ws] + [full(lg), full(lb)],
        out_specs=rowblk(d),
        compiler_params=_cparams(("parallel",)),
        name="merge",
    )(x, proj, proj, proj, ya, yb, yc, bg, *ws, lg, lb)


def _ffn_kernel(x_ref, w1_ref, w2_ref, lg_ref, lb_ref, o_ref, acc_scr, *, alpha):
    k = pl.program_id(1)

    @pl.when(k == 0)
    def _():
        acc_scr[...] = jnp.zeros(acc_scr.shape, F32)

    h = jnp.maximum(_dot(x_ref[...].astype(BF16), w1_ref[...]), 0.0)
    acc_scr[...] += _dot((h * h).astype(BF16), w2_ref[...])

    @pl.when(k == pl.num_programs(1) - 1)
    def _():
        o_ref[...] = _layer_norm(alpha * x_ref[...] + acc_scr[...], lg_ref[...], lb_ref[...])


def _ffn(x, w1, w2, ln_g, ln_b, alpha, tm=512, tf=1024):
    m, d = x.shape
    dff = w1.shape[1]
    return pl.pallas_call(
        functools.partial(_ffn_kernel, alpha=alpha),
        out_shape=jax.ShapeDtypeStruct((m, d), F32),
        grid=(m // tm, dff // tf),
        in_specs=[pl.BlockSpec((tm, d), lambda i, k: (i, 0)),
                  pl.BlockSpec((d, tf), lambda i, k: (0, k)),
                  pl.BlockSpec((tf, d), lambda i, k: (k, 0)),
                  pl.BlockSpec((1, d), lambda i, k: (0, 0)),
                  pl.BlockSpec((1, d), lambda i, k: (0, 0))],
        out_specs=pl.BlockSpec((tm, d), lambda i, k: (i, 0)),
        scratch_shapes=[pltpu.VMEM((tm, d), F32)],
        compiler_params=_cparams(("parallel", "arbitrary")),
        name="ffn",
    )(x, w1.astype(BF16), w2.astype(BF16), ln_g.reshape(1, d), ln_b.reshape(1, d))


def kernel(x, w_in, b_gate, a_lambda, a_subln_g, b_conv_w, b_a_log, b_dt_bias, b_norm_g,
           c_q_norm_g, c_kv_norm_g, c_kidx_g, c_kidx_b, c_w_uq, c_w_qidx, c_w_uk, c_w_uv,
           w_branch_a, w_branch_b, w_branch_c, w_o, ln1_g, ln1_b, w_ff1, w_ff2, ln2_g, ln2_b):
    bsz, t, d = x.shape
    depth = w_in.shape[0]
    alpha = (2 * depth) ** 0.25
    xf = x.reshape(bsz * t, d)
    for l in range(depth):
        lam_init = 0.8 - 0.6 * math.exp(-0.3 * l)
        proj = _matmul(xf, _pack_w_in(w_in[l]), F32, tm=min(1024, bsz * t), tn=512)
        y_a = _diff_attention(proj, a_lambda[l], a_subln_g[l], bsz, t, lam_init)
        y_b = _gated_deltanet(proj, b_conv_w[l], b_a_log[l], b_dt_bias[l], b_norm_g[l], bsz, t)
        y_c = _dsa_attention(proj, c_q_norm_g[l], c_kv_norm_g[l], c_kidx_g[l], c_kidx_b[l],
                             c_w_uq[l], c_w_qidx[l], c_w_uk[l], c_w_uv[l], bsz, t)
        xf = _merge(xf, proj, y_a, y_b, y_c, b_gate[l], w_branch_a[l], w_branch_b[l], w_branch_c[l],
                    w_o[l], ln1_g[l], ln1_b[l], alpha)
        xf = _ffn(xf, w_ff1[l], w_ff2[l], ln2_g[l], ln2_b[l], alpha)
    return xf.reshape(bsz, t, d)
```

```python
import functools
import math

import jax
import jax.numpy as jnp
from jax import lax
from jax.experimental import pallas as pl
from jax.experimental.pallas import tpu as pltpu

F32 = jnp.float32
BF16 = jnp.bfloat16
I32 = jnp.int32

LANES = 128
NEG = -1e30
LOG2E = 1.4426950408889634
INT_MIN = -2 ** 31
INT_MAX = 2 ** 31 - 1
VMEM_LIMIT = 56 * 1024 * 1024

A_HEADS = 4
A_QK_DIM = 64
B_HEADS = 4
B_DIM = 128
CONV_K = 4
GDN_CHUNK = 128
C_HEADS = 8
C_HEAD_DIM = 64
C_Q_LORA = 256
C_KV_LORA = 128
IDX_HEADS = 8
IDX_DIM = 64
INDEX_TOPK = 256
NORM_EPS = 1e-6

OFF_AQ, OFF_AK, OFF_AV = 0, 512, 1024
OFF_BQKV, OFF_BZ = 1536, 3072
OFF_CQ, OFF_CKV, OFF_MISC, OFF_GATES = 3584, 3840, 3968, 4096
P_COLS = 7168
MISC_WI, MISC_A, MISC_B = 64, 72, 76

_NT = (((1,), (1,)), ((), ()))
_NN = (((1,), (0,)), ((), ()))


def _cparams(sem):
    return pltpu.CompilerParams(dimension_semantics=sem, vmem_limit_bytes=VMEM_LIMIT)


def _dot(a, b, dims=_NN):
    return lax.dot_general(a, b, dims, preferred_element_type=F32)


def _layer_norm(x, g, b):
    mu = jnp.mean(x, axis=-1, keepdims=True)
    xc = x - mu
    var = jnp.mean(xc * xc, axis=-1, keepdims=True)
    return xc * lax.rsqrt(var + NORM_EPS) * g + b


def _rms(x):
    return x * lax.rsqrt(jnp.mean(x * x, axis=-1, keepdims=True) + NORM_EPS)


def _sigmoid(x):
    return 1.0 / (1.0 + jnp.exp(-x))


def _mm_kernel(x_ref, w_ref, o_ref):
    o_ref[...] = _dot(x_ref[...].astype(BF16), w_ref[...]).astype(o_ref.dtype)


def _matmul(x, w, out_dtype, tm, tn):
    m, k = x.shape
    n = w.shape[1]
    return pl.pallas_call(
        _mm_kernel,
        out_shape=jax.ShapeDtypeStruct((m, n), out_dtype),
        grid=(m // tm, n // tn),
        in_specs=[pl.BlockSpec((tm, k), lambda i, j: (i, 0)),
                  pl.BlockSpec((k, tn), lambda i, j: (0, j))],
        out_specs=pl.BlockSpec((tm, tn), lambda i, j: (i, j)),
        compiler_params=_cparams(("parallel", "parallel")),
        name="in_proj",
    )(x, w)


def _pack_w_in(w):
    d = w.shape[0]
    misc = jnp.concatenate(
        [w[:, 3976:4040], w[:, 4040:4048], w[:, 3584:3588], w[:, 3588:3592],
         jnp.zeros((d, LANES - 80), w.dtype)], axis=1)
    packed = jnp.concatenate(
        [w[:, 0:3584], w[:, 3592:3848], w[:, 3848:3976], misc, w[:, 4048:7120]], axis=1)
    return packed.astype(BF16)


def _diffattn_kernel(lam_ref, g_ref, q_ref, k_ref, v_ref, o_ref, m_scr, l_scr, acc_scr, *, tq, lam_init):
    i = pl.program_id(2)
    tiles = tq // LANES
    q = q_ref[...] * (A_QK_DIM ** -0.5 * LOG2E)
    lane = lax.broadcasted_iota(I32, q.shape, 1)
    q2 = jnp.concatenate([jnp.where(lane < A_QK_DIM, q, 0.0),
                          jnp.where(lane >= A_QK_DIM, q, 0.0)], axis=0).astype(BF16)
    m_scr[...] = jnp.full(m_scr.shape, NEG, F32)
    l_scr[...] = jnp.zeros(l_scr.shape, F32)
    acc_scr[...] = jnp.zeros(acc_scr.shape, F32)

    def block(j, diagonal):
        start = pl.multiple_of(j * tq, tq)
        kb = k_ref[pl.ds(start, tq), :].astype(BF16)
        vb = v_ref[pl.ds(start, tq), :].astype(BF16)
        s = _dot(q2, kb, _NT)
        if diagonal:
            r = lax.broadcasted_iota(I32, (tq, tq), 0)
            cc = lax.broadcasted_iota(I32, (tq, tq), 1)
            bias = jnp.where(cc <= r, 0.0, NEG)
            s = (s.reshape(2, tq, tq) + bias[None]).reshape(2 * tq, tq)
        m_prev = m_scr[...]
        m_new = jnp.maximum(m_prev, jnp.max(s, axis=-1, keepdims=True))
        alpha = jnp.exp2(m_prev - m_new)
        p = jnp.exp2(s - jnp.tile(m_new, (1, tiles)))
        l_scr[...] = alpha * l_scr[...] + jnp.sum(p, axis=-1, keepdims=True)
        m_scr[...] = m_new
        acc_scr[...] = alpha * acc_scr[...] + _dot(p.astype(BF16), vb)

    def off_diagonal(j, carry):
        block(j, False)
        return carry

    lax.fori_loop(0, i, off_diagonal, 0)
    block(i, True)

    lp = lam_ref[...]
    s1 = jnp.sum(lp[0:1, :] * lp[1:2, :], axis=-1, keepdims=True)
    s2 = jnp.sum(lp[2:3, :] * lp[3:4, :], axis=-1, keepdims=True)
    lam = jnp.exp(s1) - jnp.exp(s2) + lam_init
    o_all = acc_scr[...] / l_scr[...]
    o = o_all[0:tq] - lam * o_all[tq:2 * tq]
    o_ref[...] = _rms(o) * g_ref[...] * (1.0 - lam_init)


def _diff_attention(proj, a_lambda, subln_g, bsz, t, lam_init, tq=512):
    tq = min(tq, t)
    nq = t // tq
    qo, ko, vo = OFF_AQ // LANES, OFF_AK // LANES, OFF_AV // LANES
    once = pl.Buffered(1)
    return pl.pallas_call(
        functools.partial(_diffattn_kernel, tq=tq, lam_init=lam_init),
        out_shape=jax.ShapeDtypeStruct((bsz * t, A_HEADS * LANES), F32),
        grid=(bsz, A_HEADS, nq),
        in_specs=[
            pl.BlockSpec((4, A_QK_DIM), lambda b, h, i: (0, 0)),
            pl.BlockSpec((1, LANES), lambda b, h, i: (0, 0)),
            pl.BlockSpec((tq, LANES), lambda b, h, i: (b * nq + i, qo + h)),
            pl.BlockSpec((t, LANES), lambda b, h, i: (b, ko + h), pipeline_mode=once),
            pl.BlockSpec((t, LANES), lambda b, h, i: (b, vo + h), pipeline_mode=once),
        ],
        out_specs=pl.BlockSpec((tq, LANES), lambda b, h, i: (b * nq + i, h)),
        scratch_shapes=[pltpu.VMEM((2 * tq, LANES), F32)] * 3,
        compiler_params=_cparams(("parallel", "parallel", "arbitrary")),
        name="diff_attention",
    )(a_lambda, subln_g.reshape(1, LANES), proj, proj, proj)


def _gdn_prep_kernel(qkv_ref, misc_ref, cw_ref, gp_ref,
                     u_ref, w_ref, qd_ref, kdt_ref, attn_ref, gl_ref, xs_scr):
    c_rows = GDN_CHUNK
    n = pl.program_id(1)

    @pl.when(n == 0)
    def _():
        xs_scr[0:8, :] = jnp.zeros((8, xs_scr.shape[1]), F32)

    xs_scr[8:8 + c_rows, :] = qkv_ref[...]
    cw = cw_ref[...]
    y = cw[0:1, :] * xs_scr[5:5 + c_rows, :]
    for jj in range(1, CONV_K):
        y = y + cw[jj:jj + 1, :] * xs_scr[5 + jj:5 + jj + c_rows, :]
    xs_scr[0:8, :] = xs_scr[c_rows:c_rows + 8, :]
    y = y * _sigmoid(y)

    mi = misc_ref[...]
    gp = gp_ref[...]
    xg = mi + gp[1:2, :]
    softplus = jnp.maximum(xg, 0.0) + jnp.log(1.0 + jnp.exp(-jnp.abs(xg)))
    g_all = -jnp.exp(gp[0:1, :]) * softplus
    beta_all = _sigmoid(mi)

    r = lax.broadcasted_iota(I32, (c_rows, c_rows), 0)
    c = lax.broadcasted_iota(I32, (c_rows, c_rows), 1)
    incl = r >= c
    strict = r > c
    eye = (r == c).astype(F32)
    ones_l = incl.astype(BF16)
    g_hi = g_all.astype(BF16)
    g_r1 = g_all - g_hi.astype(F32)
    g_mid = g_r1.astype(BF16)
    g_lo = (g_r1 - g_mid.astype(F32)).astype(BF16)
    gc_all = _dot(ones_l, g_hi) + (_dot(ones_l, g_mid) + _dot(ones_l, g_lo))
    gc_t = gc_all.T

    heads = range(B_HEADS)
    sls = [slice(h * B_DIM, (h + 1) * B_DIM) for h in heads]
    qs, ks, vs, gcols, bcols, decays, kbs, khbs = [], [], [], [], [], [], [], []
    for h in heads:
        qh = y[:, h * B_DIM:(h + 1) * B_DIM]
        kh = y[:, 512 + h * B_DIM:512 + (h + 1) * B_DIM]
        qs.append(qh * lax.rsqrt(jnp.sum(qh * qh, axis=-1, keepdims=True) + NORM_EPS) * (B_DIM ** -0.5))
        kh = kh * lax.rsqrt(jnp.sum(kh * kh, axis=-1, keepdims=True) + NORM_EPS)
        ks.append(kh)
        vs.append(y[:, 1024 + h * B_DIM:1024 + (h + 1) * B_DIM])
        gcol = gc_all[:, MISC_A + h:MISC_A + h + 1]
        grow = gc_t[MISC_A + h:MISC_A + h + 1, :]
        gcols.append(gcol)
        bcols.append(beta_all[:, MISC_B + h:MISC_B + h + 1])
        decays.append(jnp.where(incl, jnp.exp(jnp.where(incl, gcol - grow, 0.0)), 0.0))
        kbs.append(kh * bcols[h])
        khbs.append(kh.astype(BF16))
    lows = [jnp.where(strict, _dot(kbs[h].astype(BF16), khbs[h], _NT) * decays[h], 0.0) for h in heads]
    attns = [_dot(qs[h].astype(BF16), khbs[h], _NT) * decays[h] for h in heads]

    same16 = (r >> 4) == (c >> 4)
    pws = [jnp.where(same16, lows[h], 0.0) for h in heads]
    invs = [eye - pws[h] for h in heads]
    for _ in range(3):
        pwbs = [pws[h].astype(BF16) for h in heads]
        pws = [_dot(pwbs[h], pwbs[h]) for h in heads]
        invs = [invs[h] + _dot(invs[h].astype(BF16), pws[h].astype(BF16)) for h in heads]
    for sh in (4, 5, 6):
        pair = ((r >> (sh + 1)) == (c >> (sh + 1))) & ((r >> sh) != (c >> sh))
        invbs = [invs[h].astype(BF16) for h in heads]
        mids = [_dot(jnp.where(pair, lows[h], 0.0).astype(BF16), invbs[h]).astype(BF16) for h in heads]
        invs = [invs[h] - _dot(invbs[h], mids[h]) for h in heads]

    egs = [jnp.exp(gcols[h]) for h in heads]
    sols = [_dot(invs[h].astype(BF16),
                 jnp.concatenate([vs[h] * bcols[h], kbs[h] * egs[h]], axis=1).astype(BF16)) for h in heads]
    for h in heads:
        sl = sls[h]
        glast = gc_all[c_rows - 1:c_rows, MISC_A + h:MISC_A + h + 1]
        u_ref[:, sl] = sols[h][:, :B_DIM]
        w_ref[:, sl] = sols[h][:, B_DIM:]
        qd_ref[:, sl] = qs[h] * egs[h]
        kdt_ref[:, sl] = (ks[h] * jnp.exp(glast - gcols[h])).T
        attn_ref[:, sl] = attns[h]
        gl_ref[:, sl] = jnp.broadcast_to(jnp.exp(glast), (8, B_DIM))


def _gdn_scan_kernel(u_ref, w_ref, qd_ref, kdt_ref, attn_ref, gl_ref, z_ref, g_ref, o_ref, s_scr):
    n = pl.program_id(1)

    @pl.when(n == 0)
    def _():
        s_scr[...] = jnp.zeros(s_scr.shape, F32)

    g = g_ref[...]
    heads = range(B_HEADS)
    sls = [slice(h * B_DIM, (h + 1) * B_DIM) for h in heads]
    ss = [s_scr[h] for h in heads]
    sbs = [ss[h].astype(BF16) for h in heads]
    wss = [_dot(w_ref[:, sls[h]].astype(BF16), sbs[h]) for h in heads]
    qss = [_dot(qd_ref[:, sls[h]].astype(BF16), sbs[h]) for h in heads]
    vbs = [(u_ref[:, sls[h]] - wss[h]).astype(BF16) for h in heads]
    os_ = [qss[h] + _dot(attn_ref[:, sls[h]].astype(BF16), vbs[h]) for h in heads]
    kvs = [_dot(kdt_ref[:, sls[h]].astype(BF16), vbs[h]) for h in heads]
    for h in heads:
        s_scr[h] = ss[h] * gl_ref[0:1, sls[h]] + kvs[h]
        z = z_ref[:, sls[h]]
        o_ref[:, sls[h]] = _rms(os_[h]) * g * (z * _sigmoid(z))


def _gated_deltanet(proj, conv_w, a_log, dt_bias, norm_g, bsz, t):
    m = bsz * t
    c_rows = GDN_CHUNK
    n = t // c_rows
    width = B_HEADS * B_DIM
    gp = jnp.zeros((8, LANES), F32)
    gp = gp.at[0, MISC_A:MISC_A + B_HEADS].set(a_log.astype(F32))
    gp = gp.at[1, MISC_A:MISC_A + B_HEADS].set(dt_bias.astype(F32))
    row = lambda b, i: (b * n + i, 0)
    big = pl.BlockSpec((c_rows, width), row)
    small = pl.BlockSpec((8, width), row)
    u, w, qd, kdt, attn, gl = pl.pallas_call(
        _gdn_prep_kernel,
        out_shape=[jax.ShapeDtypeStruct((m, width), F32)] * 5
        + [jax.ShapeDtypeStruct((bsz * n * 8, width), F32)],
        grid=(bsz, n),
        in_specs=[pl.BlockSpec((c_rows, 3 * width), lambda b, i: (b * n + i, OFF_BQKV // (3 * width))),
                  pl.BlockSpec((c_rows, LANES), lambda b, i: (b * n + i, OFF_MISC // LANES)),
                  pl.BlockSpec((CONV_K, 3 * width), lambda b, i: (0, 0)),
                  pl.BlockSpec((8, LANES), lambda b, i: (0, 0))],
        out_specs=[big] * 5 + [small],
        scratch_shapes=[pltpu.VMEM((c_rows + 8, 3 * width), F32)],
        compiler_params=_cparams(("parallel", "arbitrary")),
        name="gdn_prep",
    )(proj, proj, conv_w, gp)
    return pl.pallas_call(
        _gdn_scan_kernel,
        out_shape=jax.ShapeDtypeStruct((m, width), F32),
        grid=(bsz, n),
        in_specs=[big] * 5 + [small,
                              pl.BlockSpec((c_rows, width), lambda b, i: (b * n + i, OFF_BZ // width)),
                              pl.BlockSpec((1, B_DIM), lambda b, i: (0, 0))],
        out_specs=big,
        scratch_shapes=[pltpu.VMEM((B_HEADS, B_DIM, B_DIM), F32)],
        compiler_params=_cparams(("parallel", "arbitrary")),
        name="gdn_scan",
    )(u, w, qd, kdt, attn, gl, proj, norm_g.reshape(1, B_DIM))


def _dsa_prep_kernel(cq_ref, ckv_ref, misc_ref, qg_ref, kvg_ref, kig_ref, kib_ref,
                     wuq_ref, wuk_ref, wqi_ref, qa_ref, qi_ref, ckvn_ref, kpad_ref, wi_ref):
    tm = cq_ref.shape[0]
    ql = (_rms(cq_ref[...]) * qg_ref[...]).astype(BF16)
    qh = _dot(ql, wuq_ref[...]).astype(BF16)
    qa = (_dot(qh, wuk_ref[...]) * (C_HEAD_DIM ** -0.5 * LOG2E)).astype(BF16)
    qi = _dot(ql, wqi_ref[...]).astype(BF16)
    for h in range(C_HEADS):
        qa_ref[h * tm:(h + 1) * tm, :] = qa[:, h * LANES:(h + 1) * LANES]
        qi_ref[h * tm:(h + 1) * tm, :] = qi[:, h * LANES:(h + 1) * LANES]
    ckvn_ref[...] = (_rms(ckv_ref[...]) * kvg_ref[...]).astype(BF16)
    mi = misc_ref[...]
    lane = lax.broadcasted_iota(I32, mi.shape, 1)
    isk = lane < IDX_DIM
    mu = jnp.sum(jnp.where(isk, mi, 0.0), axis=-1, keepdims=True) * (1.0 / IDX_DIM)
    xc = jnp.where(isk, mi - mu, 0.0)
    var = jnp.sum(xc * xc, axis=-1, keepdims=True) * (1.0 / IDX_DIM)
    kn = xc * lax.rsqrt(var + NORM_EPS) * kig_ref[...] + kib_ref[...]
    kpad_ref[...] = jnp.where(isk, kn, 0.0).astype(BF16)
    wi_ref[...] = mi * (IDX_HEADS ** -0.5 * IDX_DIM ** -0.5)


def _float_key(x):
    b = lax.bitcast_convert_type(x, I32)
    return jnp.where(b >= 0, b, b ^ INT_MAX)


def _bit_transpose32(words):
    w = list(words)
    j, m = 16, 0x0000FFFF
    while j:
        k = 0
        while k < 32:
            t = (w[k] ^ lax.shift_right_logical(w[k + j], j)) & m
            w[k] = w[k] ^ t
            w[k + j] = w[k + j] ^ lax.shift_left(t, j)
            k = (k + j + 1) & ~j
        j >>= 1
        m = (m ^ (m << j)) & 0x7FFFFFFF
    return w


def _dsa_main_kernel(qa_ref, qi_ref, wi_ref, kpad_ref, ckv_ref, wuv_ref, o_ref,
                     key_scr, alive_scr, gt_scr, sel_scr, m_scr, l_scr, acc_scr,
                     *, tq, tk, n_sel, idx_bits):
    i = pl.program_id(1)
    nvis = ((i + 1) * tq + tk - 1) // tk
    tiles = tk // LANES
    group = 32 // tiles
    span = 32 * LANES
    ngrp = (nvis + group - 1) // group
    rows = i * tq + lax.broadcasted_iota(I32, (tq, tk), 0)
    col0 = lax.broadcasted_iota(I32, (tq, tk), 1)
    lane = lax.broadcasted_iota(I32, (tq, LANES), 1)
    wi = wi_ref[...]

    def score_block(jb, carry):
        start = pl.multiple_of(jb * tk, tk)
        kblk = kpad_ref[pl.ds(start, tk), :]
        s = _dot(qi_ref[...], kblk, _NT)
        acc = None
        for h in range(IDX_HEADS):
            term = jnp.maximum(s[h * tq:(h + 1) * tq], 0.0) * wi[:, MISC_WI + h:MISC_WI + h + 1]
            acc = term if acc is None else acc + term
        key = jnp.where(col0 + start <= rows, _float_key(acc) ^ INT_MIN, 0)
        for tt in range(tiles):
            key_scr[jb * tiles + tt] = key[:, tt * LANES:(tt + 1) * LANES]
        return carry

    lax.fori_loop(0, nvis, score_block, 0)

    def fill_tile(kt, carry):
        key_scr[kt] = jnp.zeros((tq, LANES), I32)
        return carry

    lax.fori_loop(nvis * tiles, ngrp * 32, fill_tile, 0)

    def plane_group(g, carry):
        def plane_rows(rg, carry2):
            r0 = pl.multiple_of(rg * 8, 8)
            planes = _bit_transpose32([key_scr[g * 32 + k, pl.ds(r0, 8), :] for k in range(32)])
            for b in range(32):
                key_scr[g * 32 + b, pl.ds(r0, 8), :] = planes[b]
            return carry2
        lax.fori_loop(0, tq // 8, plane_rows, 0)
        alive_scr[g] = jnp.full((tq, LANES), -1, I32)
        gt_scr[g] = jnp.zeros((tq, LANES), I32)
        return carry

    lax.fori_loop(0, ngrp, plane_group, 0)

    def popcount_rows(mask_fn):
        def body(g, acc):
            return acc + lax.population_count(mask_fn(g))
        acc = lax.fori_loop(0, ngrp, body, jnp.zeros((tq, LANES), I32))
        return jnp.sum(acc.astype(F32), axis=-1, keepdims=True)

    def radix_step(it, need):
        ones = popcount_rows(lambda g: alive_scr[g] & key_scr[g * 32 + it])
        take = ones >= need
        flip = jnp.where(take, 0, -1)

        def update(g, c):
            a = alive_scr[g]
            p = key_scr[g * 32 + it]
            gt_scr[g] = gt_scr[g] | (a & p & flip)
            alive_scr[g] = a & (p ^ flip)
            return c
        lax.fori_loop(0, ngrp, update, 0)
        return jnp.where(take, need, need - ones)

    need = lax.fori_loop(0, 32, radix_step, jnp.full((tq, 1), float(n_sel), F32))
    ties = popcount_rows(lambda g: alive_scr[g])

    def cols_below(g, x):
        nk = jnp.clip((x - g * span - lane + (LANES - 1)) >> 7, 0, 32)
        top = ~lax.shift_right_logical(jnp.full((tq, LANES), -1, I32), jnp.minimum(nk, 31))
        return jnp.where(nk >= 32, -1, top)

    qpos = i * tq + lax.broadcasted_iota(I32, (tq, 1), 0)
    few = qpos < n_sel
    overflow = jnp.where(few, 0.0, ties - need)
    xcut_all = jnp.full((tq, 1), 1 << 30, I32)

    def resolve_ties():
        def cut_step(it, cut):
            cand = cut | lax.shift_left(jnp.int32(1), idx_bits - 1 - it)
            below = popcount_rows(lambda g: alive_scr[g] & cols_below(g, cand))
            return jnp.where(below < need, cand, cut)

        cut = lax.fori_loop(0, idx_bits, cut_step, jnp.zeros((tq, 1), I32))
        return jnp.where(overflow > 0.0, cut + 1, xcut_all)

    xcut = lax.cond(jnp.max(overflow) > 0.0, resolve_ties, lambda: xcut_all)

    def select_group(g, carry):
        chosen = gt_scr[g] | (alive_scr[g] & cols_below(g, xcut))
        sel_scr[g] = jnp.where(few, cols_below(g, qpos + 1), chosen)
        return carry

    lax.fori_loop(0, ngrp, select_group, 0)

    m_scr[...] = jnp.full(m_scr.shape, NEG, F32)
    l_scr[...] = jnp.zeros(l_scr.shape, F32)
    acc_scr[...] = jnp.zeros(acc_scr.shape, F32)

    def attend_block(jb, carry):
        start = pl.multiple_of(jb * tk, tk)
        sel = sel_scr[jb // group]
        k0 = (jb % group) * tiles
        bias = jnp.concatenate(
            [jnp.where(jnp.left_shift(sel, k0 + tt) < 0, 0.0, NEG) for tt in range(tiles)], axis=1)
        ckv = ckv_ref[pl.ds(start, tk), :]
        s = _dot(qa_ref[...], ckv, _NT)
        s = (s.reshape(C_HEADS, tq, tk) + bias[None]).reshape(C_HEADS * tq, tk)
        m_prev = m_scr[...]
        m_new = jnp.maximum(m_prev, jnp.max(s, axis=-1, keepdims=True))
        alpha = jnp.exp2(m_prev - m_new)
        p = jnp.exp2(s - jnp.tile(m_new, (1, tiles)))
        l_scr[...] = alpha * l_scr[...] + jnp.sum(p, axis=-1, keepdims=True)
        m_scr[...] = m_new
        acc_scr[...] = alpha * acc_scr[...] + _dot(p.astype(BF16), ckv)
        return carry

    lax.fori_loop(0, nvis, attend_block, 0)
    o_all = acc_scr[...] / l_scr[...]
    o_lat = jnp.concatenate([o_all[h * tq:(h + 1) * tq] for h in range(C_HEADS)], axis=1)
    o_ref[...] = _dot(o_lat.astype(BF16), wuv_ref[...])


def _dsa_attention(proj, q_norm_g, kv_norm_g, kidx_g, kidx_b, w_uq, w_qidx, w_uk, w_uv,
                   bsz, t, tq=256, tk=512):
    m = bsz * t
    tq = min(tq, t)
    tk = min(tk, t)
    tm = tq
    hr = C_HEADS * C_KV_LORA
    wuq = w_uq.reshape(C_Q_LORA, C_HEADS * C_HEAD_DIM).astype(BF16)
    eye_h = jnp.eye(C_HEADS, dtype=F32)
    wuk_bd = jnp.einsum('rhd,hg->hdgr', w_uk, eye_h).reshape(C_HEADS * C_HEAD_DIM, hr).astype(BF16)
    wuv_bd = jnp.einsum('rhd,hg->hrgd', w_uv, eye_h).reshape(hr, C_HEADS * C_HEAD_DIM).astype(BF16)
    wqi = jnp.pad(w_qidx, ((0, 0), (0, 0), (0, LANES - IDX_DIM))).reshape(C_Q_LORA, IDX_HEADS * LANES).astype(BF16)
    pad_row = lambda v: jnp.pad(v.astype(F32), (0, LANES - IDX_DIM)).reshape(1, LANES)
    full = lambda shape: pl.BlockSpec(shape, lambda i: (0, 0))
    qa, qi, ckvn, kpad, wi = pl.pallas_call(
        _dsa_prep_kernel,
        out_shape=[jax.ShapeDtypeStruct((m * C_HEADS, LANES), BF16), jax.ShapeDtypeStruct((m * IDX_HEADS, LANES), BF16),
                   jax.ShapeDtypeStruct((m, LANES), BF16), jax.ShapeDtypeStruct((m, LANES), BF16),
                   jax.ShapeDtypeStruct((m, LANES), F32)],
        grid=(m // tm,),
        in_specs=[pl.BlockSpec((tm, C_Q_LORA), lambda i: (i, OFF_CQ // C_Q_LORA)),
                  pl.BlockSpec((tm, LANES), lambda i: (i, OFF_CKV // LANES)),
                  pl.BlockSpec((tm, LANES), lambda i: (i, OFF_MISC // LANES)),
                  full((1, C_Q_LORA)), full((1, LANES)), full((1, LANES)), full((1, LANES)),
                  full(wuq.shape), full(wuk_bd.shape), full(wqi.shape)],
        out_specs=[pl.BlockSpec((C_HEADS * tm, LANES), lambda i: (i, 0)),
                   pl.BlockSpec((IDX_HEADS * tm, LANES), lambda i: (i, 0)),
                   pl.BlockSpec((tm, LANES), lambda i: (i, 0)), pl.BlockSpec((tm, LANES), lambda i: (i, 0)),
                   pl.BlockSpec((tm, LANES), lambda i: (i, 0))],
        compiler_params=_cparams(("parallel",)),
        name="dsa_prep",
    )(proj, proj, proj, q_norm_g.reshape(1, C_Q_LORA), kv_norm_g.reshape(1, LANES),
      pad_row(kidx_g), pad_row(kidx_b), wuq, wuk_bd, wqi)

    nq = t // tq
    n_sel = min(INDEX_TOPK, t // 4)
    idx_bits = max(1, int(math.ceil(math.log2(t))))
    ngrp = -(-t // (32 * LANES))
    blk = lambda rows, w: pl.BlockSpec((rows, w), lambda b, i: (b * nq + i, 0))
    once = pl.Buffered(1)
    seq = lambda w: pl.BlockSpec((t, w), lambda b, i: (b, 0), pipeline_mode=once)
    return pl.pallas_call(
        functools.partial(_dsa_main_kernel, tq=tq, tk=tk, n_sel=n_sel, idx_bits=idx_bits),
        out_shape=jax.ShapeDtypeStruct((m, C_HEADS * C_HEAD_DIM), F32),
        grid=(bsz, nq),
        in_specs=[blk(C_HEADS * tq, LANES), blk(IDX_HEADS * tq, LANES), blk(tq, LANES), seq(LANES), seq(LANES),
                  pl.BlockSpec(wuv_bd.shape, lambda b, i: (0, 0), pipeline_mode=once)],
        out_specs=blk(tq, C_HEADS * C_HEAD_DIM),
        scratch_shapes=[pltpu.VMEM((ngrp * 32, tq, LANES), I32)]
        + [pltpu.VMEM((ngrp, tq, LANES), I32)] * 3
        + [pltpu.VMEM((C_HEADS * tq, LANES), F32)] * 3,
        compiler_params=_cparams(("parallel", "arbitrary")),
        name="dsa_main",
    )(qa, qi, wi, kpad, ckvn, wuv_bd)


def _merge_kernel(x_ref, g0_ref, g1_ref, g2_ref, ya_ref, yb_ref, yc_ref, bg_ref,
                  wa_ref, wb_ref, wc_ref, wo_ref, lg_ref, lb_ref, o_ref, *, alpha):
    bg = bg_ref[...]
    merged = None
    for idx, (g_ref, y_ref, w_ref) in enumerate(
            ((g0_ref, ya_ref, wa_ref), (g1_ref, yb_ref, wb_ref), (g2_ref, yc_ref, wc_ref))):
        gate = _sigmoid(g_ref[...] + bg[idx:idx + 1, :])
        term = gate * _dot(y_ref[...].astype(BF16), w_ref[...])
        merged = term if merged is None else merged + term
    y = alpha * x_ref[...] + _dot(merged.astype(BF16), wo_ref[...])
    o_ref[...] = _layer_norm(y, lg_ref[...], lb_ref[...])


def _merge(x, proj, ya, yb, yc, b_gate, wa, wb, wc, wo, ln_g, ln_b, alpha, tm=256):
    m, d = x.shape
    gate_blk = lambda k: pl.BlockSpec((tm, d), lambda i: (i, OFF_GATES // d + k))
    rowblk = lambda w: pl.BlockSpec((tm, w), lambda i: (i, 0))
    full = lambda a: pl.BlockSpec(a.shape, lambda i: (0, 0))
    bg = b_gate.reshape(3, d)
    ws = [wa.astype(BF16), wb.astype(BF16), wc.astype(BF16), wo.astype(BF16)]
    lg, lb = ln_g.reshape(1, d), ln_b.reshape(1, d)
    return pl.pallas_call(
        functools.partial(_merge_kernel, alpha=alpha),
        out_shape=jax.ShapeDtypeStruct((m, d), F32),
        grid=(m // tm,),
        in_specs=[rowblk(d), gate_blk(0), gate_blk(1), gate_blk(2),
                  rowblk(ya.shape[1]), rowblk(yb.shape[1]), rowblk(yc.shape[1]), full(bg)]
        + [full(w) for w in ws] + [full(lg), full(lb)],
        out_specs=rowblk(d),
        compiler_params=_cparams(("parallel",)),
        name="merge",
    )(x, proj, proj, proj, ya, yb, yc, bg, *ws, lg, lb)


def _ffn_kernel(x_ref, w1_ref, w2_ref, lg_ref, lb_ref, o_ref, acc_scr, *, alpha):
    k = pl.program_id(1)

    @pl.when(k == 0)
    def _():
        acc_scr[...] = jnp.zeros(acc_scr.shape, F32)

    h = jnp.maximum(_dot(x_ref[...].astype(BF16), w1_ref[...]), 0.0)
    acc_scr[...] += _dot((h * h).astype(BF16), w2_ref[...])

    @pl.when(k == pl.num_programs(1) - 1)
    def _():
        o_ref[...] = _layer_norm(alpha * x_ref[...] + acc_scr[...], lg_ref[...], lb_ref[...])


def _ffn(x, w1, w2, ln_g, ln_b, alpha, tm=512, tf=1024):
    m, d = x.shape
    dff = w1.shape[1]
    return pl.pallas_call(
        functools.partial(_ffn_kernel, alpha=alpha),
        out_shape=jax.ShapeDtypeStruct((m, d), F32),
        grid=(m // tm, dff // tf),
        in_specs=[pl.BlockSpec((tm, d), lambda i, k: (i, 0)),
                  pl.BlockSpec((d, tf), lambda i, k: (0, k)),
                  pl.BlockSpec((tf, d), lambda i, k: (k, 0)),
                  pl.BlockSpec((1, d), lambda i, k: (0, 0)),
                  pl.BlockSpec((1, d), lambda i, k: (0, 0))],
        out_specs=pl.BlockSpec((tm, d), lambda i, k: (i, 0)),
        scratch_shapes=[pltpu.VMEM((tm, d), F32)],
        compiler_params=_cparams(("parallel", "arbitrary")),
        name="ffn",
    )(x, w1.astype(BF16), w2.astype(BF16), ln_g.reshape(1, d), ln_b.reshape(1, d))


def kernel(x, w_in, b_gate, a_lambda, a_subln_g, b_conv_w, b_a_log, b_dt_bias, b_norm_g,
           c_q_norm_g, c_kv_norm_g, c_kidx_g, c_kidx_b, c_w_uq, c_w_qidx, c_w_uk, c_w_uv,
           w_branch_a, w_branch_b, w_branch_c, w_o, ln1_g, ln1_b, w_ff1, w_ff2, ln2_g, ln2_b):
    bsz, t, d = x.shape
    depth = w_in.shape[0]
    alpha = (2 * depth) ** 0.25
    xf = x.reshape(bsz * t, d)
    for l in range(depth):
        lam_init = 0.8 - 0.6 * math.exp(-0.3 * l)
        proj = _matmul(xf, _pack_w_in(w_in[l]), F32, tm=min(1024, bsz * t), tn=512)
        y_a = _diff_attention(proj, a_lambda[l], a_subln_g[l], bsz, t, lam_init)
        y_b = _gated_deltanet(proj, b_conv_w[l], b_a_log[l], b_dt_bias[l], b_norm_g[l], bsz, t)
        y_c = _dsa_attention(proj, c_q_norm_g[l], c_kv_norm_g[l], c_kidx_g[l], c_kidx_b[l],
                             c_w_uq[l], c_w_qidx[l], c_w_uk[l], c_w_uv[l], bsz, t)
        xf = _merge(xf, proj, y_a, y_b, y_c, b_gate[l], w_branch_a[l], w_branch_b[l], w_branch_c[l],
                    w_o[l], ln1_g[l], ln1_b[l], alpha)
        xf = _ffn(xf, w_ff1[l], w_ff2[l], ln2_g[l], ln2_b[l], alpha)
    return xf.reshape(bsz, t, d)
```

```python
import functools
import math

import jax
import jax.numpy as jnp
from jax import lax
from jax.experimental import pallas as pl
from jax.experimental.pallas import tpu as pltpu

F32 = jnp.float32
BF16 = jnp.bfloat16
I32 = jnp.int32

LANES = 128
NEG = -1e30
LOG2E = 1.4426950408889634
INT_MIN = -2 ** 31
INT_MAX = 2 ** 31 - 1
VMEM_LIMIT = 56 * 1024 * 1024

A_HEADS = 4
A_QK_DIM = 64
B_HEADS = 4
B_DIM = 128
CONV_K = 4
GDN_CHUNK = 128
GDN_SUB = 4
C_HEADS = 8
C_HEAD_DIM = 64
C_Q_LORA = 256
C_KV_LORA = 128
IDX_HEADS = 8
IDX_DIM = 64
INDEX_TOPK = 256
NORM_EPS = 1e-6
ATT_CHAINS = 2

OFF_AQ, OFF_AK, OFF_AV = 0, 512, 1024
OFF_BQKV, OFF_BZ = 1536, 3072
OFF_CQ, OFF_CKV, OFF_MISC, OFF_GATES = 3584, 3840, 3968, 4096
P_COLS = 7168
MISC_WI, MISC_A, MISC_B = 64, 72, 76

_NT = (((1,), (1,)), ((), ()))
_NN = (((1,), (0,)), ((), ()))


def _cparams(sem):
    return pltpu.CompilerParams(dimension_semantics=sem, vmem_limit_bytes=VMEM_LIMIT)


def _dot(a, b, dims=_NN):
    return lax.dot_general(a, b, dims, preferred_element_type=F32)


def _for_blocks(n, body):
    def pair(jj, carry):
        body(2 * jj)
        body(2 * jj + 1)
        return carry

    lax.fori_loop(0, n // 2, pair, 0)

    @pl.when(n % 2 == 1)
    def _():
        body(n - 1)


def _layer_norm(x, g, b):
    mu = jnp.mean(x, axis=-1, keepdims=True)
    xc = x - mu
    var = jnp.mean(xc * xc, axis=-1, keepdims=True)
    return xc * lax.rsqrt(var + NORM_EPS) * g + b


def _rms(x):
    return x * lax.rsqrt(jnp.mean(x * x, axis=-1, keepdims=True) + NORM_EPS)


def _sigmoid(x):
    return 1.0 / (1.0 + jnp.exp(-x))


def _mm_kernel(x_ref, w_ref, o_ref):
    o_ref[...] = _dot(x_ref[...].astype(BF16), w_ref[...]).astype(o_ref.dtype)


def _matmul(x, w, out_dtype, tm, tn):
    m, k = x.shape
    n = w.shape[1]
    return pl.pallas_call(
        _mm_kernel,
        out_shape=jax.ShapeDtypeStruct((m, n), out_dtype),
        grid=(m // tm, n // tn),
        in_specs=[pl.BlockSpec((tm, k), lambda i, j: (i, 0)),
                  pl.BlockSpec((k, tn), lambda i, j: (0, j))],
        out_specs=pl.BlockSpec((tm, tn), lambda i, j: (i, j)),
        compiler_params=_cparams(("parallel", "parallel")),
        name="in_proj",
    )(x, w)


def _pack_w_in(w):
    d = w.shape[0]
    misc = jnp.concatenate(
        [w[:, 3976:4040], w[:, 4040:4048], w[:, 3584:3588], w[:, 3588:3592],
         jnp.zeros((d, LANES - 80), w.dtype)], axis=1)
    packed = jnp.concatenate(
        [w[:, 0:3584], w[:, 3592:3848], w[:, 3848:3976], misc, w[:, 4048:7120]], axis=1)
    return packed.astype(BF16)


def _diffattn_kernel(lam_ref, g_ref, q_ref, k_ref, v_ref, o_ref, m_scr, l_scr, acc_scr, *, tq, tk, lam_init):
    i = pl.program_id(2)
    tiles = tk // LANES
    ratio = tq // tk
    q = q_ref[...] * (A_QK_DIM ** -0.5 * LOG2E)
    lane = lax.broadcasted_iota(I32, q.shape, 1)
    q2 = jnp.concatenate([jnp.where(lane < A_QK_DIM, q, 0.0),
                          jnp.where(lane >= A_QK_DIM, q, 0.0)], axis=0).astype(BF16)
    m_scr[...] = jnp.full(m_scr.shape, NEG, F32)
    l_scr[...] = jnp.zeros(l_scr.shape, F32)
    acc_scr[...] = jnp.zeros(acc_scr.shape, F32)

    def block(j, diagonal):
        start = pl.multiple_of(j * tk, tk)
        kb = k_ref[pl.ds(start, tk), :].astype(BF16)
        vb = v_ref[pl.ds(start, tk), :].astype(BF16)
        rs = [slice(0, tq), slice(tq, 2 * tq)]
        ss = [_dot(q2[r_], kb, _NT) for r_ in rs]
        if diagonal:
            r = lax.broadcasted_iota(I32, (tq, tk), 0)
            cc = lax.broadcasted_iota(I32, (tq, tk), 1)
            bias = jnp.where(cc + (j - i * ratio) * tk <= r, 0.0, NEG)
            ss = [s + bias for s in ss]
        m_prevs = [m_scr[r_, :] for r_ in rs]
        m_news = [jnp.maximum(mp, jnp.max(s, axis=-1, keepdims=True)) for mp, s in zip(m_prevs, ss)]
        alphas = [jnp.exp2(mp - mn) for mp, mn in zip(m_prevs, m_news)]
        ps = [jnp.exp2(s - jnp.tile(mn, (1, tiles))) for s, mn in zip(ss, m_news)]
        pvs = [_dot(p.astype(BF16), vb) for p in ps]
        for r_, mn, al, p, pv in zip(rs, m_news, alphas, ps, pvs):
            l_scr[r_, :] = al * l_scr[r_, :] + jnp.sum(p, axis=-1, keepdims=True)
            m_scr[r_, :] = mn
            acc_scr[r_, :] = al * acc_scr[r_, :] + pv

    _for_blocks(i * ratio, lambda j: block(j, False))
    for d in range(ratio):
        block(i * ratio + d, True)

    lp = lam_ref[...]
    s1 = jnp.sum(lp[0:1, :] * lp[1:2, :], axis=-1, keepdims=True)
    s2 = jnp.sum(lp[2:3, :] * lp[3:4, :], axis=-1, keepdims=True)
    lam = jnp.exp(s1) - jnp.exp(s2) + lam_init
    o_all = acc_scr[...] / l_scr[...]
    o = o_all[0:tq] - lam * o_all[tq:2 * tq]
    o_ref[...] = _rms(o) * g_ref[...] * (1.0 - lam_init)


def _diff_attention(proj, a_lambda, subln_g, bsz, t, lam_init, tq=512, tk=512):
    tq = min(tq, t)
    tk = min(tk, tq)
    nq = t // tq
    qo, ko, vo = OFF_AQ // LANES, OFF_AK // LANES, OFF_AV // LANES
    once = pl.Buffered(1)
    return pl.pallas_call(
        functools.partial(_diffattn_kernel, tq=tq, tk=tk, lam_init=lam_init),
        out_shape=jax.ShapeDtypeStruct((bsz * t, A_HEADS * LANES), F32),
        grid=(bsz, A_HEADS, nq),
        in_specs=[
            pl.BlockSpec((4, A_QK_DIM), lambda b, h, i: (0, 0)),
            pl.BlockSpec((1, LANES), lambda b, h, i: (0, 0)),
            pl.BlockSpec((tq, LANES), lambda b, h, i: (b * nq + i, qo + h)),
            pl.BlockSpec((t, LANES), lambda b, h, i: (b, ko + h), pipeline_mode=once),
            pl.BlockSpec((t, LANES), lambda b, h, i: (b, vo + h), pipeline_mode=once),
        ],
        out_specs=pl.BlockSpec((tq, LANES), lambda b, h, i: (b * nq + i, h)),
        scratch_shapes=[pltpu.VMEM((2 * tq, LANES), F32)] * 3,
        compiler_params=_cparams(("parallel", "parallel", "arbitrary")),
        name="diff_attention",
    )(a_lambda, subln_g.reshape(1, LANES), proj, proj, proj)


def _gdn_prep_kernel(qkv_ref, misc_ref, cw_ref, gp_ref,
                     u_ref, w_ref, qd_ref, kdt_ref, attn_ref, gl_ref, xs_scr):
    c_rows = GDN_CHUNK
    rows = c_rows * GDN_SUB
    n = pl.program_id(1)

    @pl.when(n == 0)
    def _():
        xs_scr[0:8, :] = jnp.zeros((8, xs_scr.shape[1]), F32)

    xs_scr[8:8 + rows, :] = qkv_ref[...]
    cw = cw_ref[...]
    y = cw[0:1, :] * xs_scr[5:5 + rows, :]
    for jj in range(1, CONV_K):
        y = y + cw[jj:jj + 1, :] * xs_scr[5 + jj:5 + jj + rows, :]
    xs_scr[0:8, :] = xs_scr[rows:rows + 8, :]
    y = y * _sigmoid(y)

    mi = misc_ref[...]
    gp = gp_ref[...]
    xg = mi + gp[1:2, :]
    softplus = jnp.maximum(xg, 0.0) + jnp.log(1.0 + jnp.exp(-jnp.abs(xg)))
    g_all = -jnp.exp(gp[0:1, :]) * softplus
    beta_all = _sigmoid(mi)

    r = lax.broadcasted_iota(I32, (c_rows, c_rows), 0)
    c = lax.broadcasted_iota(I32, (c_rows, c_rows), 1)
    incl = r >= c
    strict = r > c
    eye = (r == c).astype(F32)
    ones_l = incl.astype(BF16)
    g_hi = g_all.astype(BF16)
    g_r1 = g_all - g_hi.astype(F32)
    g_mid = g_r1.astype(BF16)
    g_lo = (g_r1 - g_mid.astype(F32)).astype(BF16)
    subs = [slice(sc * c_rows, (sc + 1) * c_rows) for sc in range(GDN_SUB)]
    gcs = [_dot(ones_l, g_hi[rs_]) + (_dot(ones_l, g_mid[rs_]) + _dot(ones_l, g_lo[rs_]))
           for rs_ in subs]
    gcts = [gc.T for gc in gcs]

    heads = range(GDN_SUB * B_HEADS)
    qs, ks, vs, gcols, bcols, decays, kbs, khbs = [], [], [], [], [], [], [], []
    for inst in heads:
        sc, h = divmod(inst, B_HEADS)
        rs_ = subs[sc]
        qh = y[rs_, h * B_DIM:(h + 1) * B_DIM]
        kh = y[rs_, 512 + h * B_DIM:512 + (h + 1) * B_DIM]
        qs.append(qh * lax.rsqrt(jnp.sum(qh * qh, axis=-1, keepdims=True) + NORM_EPS) * (B_DIM ** -0.5))
        kh = kh * lax.rsqrt(jnp.sum(kh * kh, axis=-1, keepdims=True) + NORM_EPS)
        ks.append(kh)
        vs.append(y[rs_, 1024 + h * B_DIM:1024 + (h + 1) * B_DIM])
        gcol = gcs[sc][:, MISC_A + h:MISC_A + h + 1]
        grow = gcts[sc][MISC_A + h:MISC_A + h + 1, :]
        gcols.append(gcol)
        bcols.append(beta_all[rs_, MISC_B + h:MISC_B + h + 1])
        decays.append(jnp.where(incl, jnp.exp(jnp.where(incl, gcol - grow, 0.0)), 0.0))
        kbs.append(kh * bcols[inst])
        khbs.append(kh.astype(BF16))
    lows = [jnp.where(strict, _dot(kbs[h].astype(BF16), khbs[h], _NT) * decays[h], 0.0) for h in heads]
    attns = [_dot(qs[h].astype(BF16), khbs[h], _NT) * decays[h] for h in heads]

    same16 = (r >> 4) == (c >> 4)
    pws = [jnp.where(same16, lows[h], 0.0) for h in heads]
    invs = [eye - pws[h] for h in heads]
    for _ in range(3):
        pwbs = [pws[h].astype(BF16) for h in heads]
        pws = [_dot(pwbs[h], pwbs[h]) for h in heads]
        invs = [invs[h] + _dot(invs[h].astype(BF16), pws[h].astype(BF16)) for h in heads]
    for sh in (4, 5, 6):
        pair = ((r >> (sh + 1)) == (c >> (sh + 1))) & ((r >> sh) != (c >> sh))
        invbs = [invs[h].astype(BF16) for h in heads]
        mids = [_dot(jnp.where(pair, lows[h], 0.0).astype(BF16), invbs[h]).astype(BF16) for h in heads]
        invs = [invs[h] - _dot(invbs[h], mids[h]) for h in heads]

    egs = [jnp.exp(gcols[h]) for h in heads]
    sols = [_dot(invs[h].astype(BF16),
                 jnp.concatenate([vs[h] * bcols[h], kbs[h] * egs[h]], axis=1).astype(BF16)) for h in heads]
    for inst in heads:
        sc, h = divmod(inst, B_HEADS)
        rs_ = subs[sc]
        sl = slice(h * B_DIM, (h + 1) * B_DIM)
        glast = gcs[sc][c_rows - 1:c_rows, MISC_A + h:MISC_A + h + 1]
        u_ref[rs_, sl] = sols[inst][:, :B_DIM]
        w_ref[rs_, sl] = sols[inst][:, B_DIM:]
        qd_ref[rs_, sl] = qs[inst] * egs[inst]
        kdt_ref[rs_, sl] = (ks[inst] * jnp.exp(glast - gcols[inst])).T
        attn_ref[rs_, sl] = attns[inst]
        gl_ref[sc * 8:(sc + 1) * 8, sl] = jnp.broadcast_to(jnp.exp(glast), (8, B_DIM))


def _gdn_scan_kernel(u_ref, w_ref, qd_ref, kdt_ref, attn_ref, gl_ref, z_ref, g_ref, o_ref, s_scr):
    n = pl.program_id(1)

    @pl.when(n == 0)
    def _():
        s_scr[...] = jnp.zeros(s_scr.shape, F32)

    g = g_ref[...]
    heads = range(B_HEADS)
    sls = [slice(h * B_DIM, (h + 1) * B_DIM) for h in heads]
    ss = [s_scr[h] for h in heads]
    sbs = [ss[h].astype(BF16) for h in heads]
    wss = [_dot(w_ref[:, sls[h]].astype(BF16), sbs[h]) for h in heads]
    qss = [_dot(qd_ref[:, sls[h]].astype(BF16), sbs[h]) for h in heads]
    vbs = [(u_ref[:, sls[h]] - wss[h]).astype(BF16) for h in heads]
    os_ = [qss[h] + _dot(attn_ref[:, sls[h]].astype(BF16), vbs[h]) for h in heads]
    kvs = [_dot(kdt_ref[:, sls[h]].astype(BF16), vbs[h]) for h in heads]
    for h in heads:
        s_scr[h] = ss[h] * gl_ref[0:1, sls[h]] + kvs[h]
        z = z_ref[:, sls[h]]
        o_ref[:, sls[h]] = _rms(os_[h]) * g * (z * _sigmoid(z))


def _gated_deltanet(proj, conv_w, a_log, dt_bias, norm_g, bsz, t):
    m = bsz * t
    c_rows = GDN_CHUNK
    n = t // c_rows
    width = B_HEADS * B_DIM
    gp = jnp.zeros((8, LANES), F32)
    gp = gp.at[0, MISC_A:MISC_A + B_HEADS].set(a_log.astype(F32))
    gp = gp.at[1, MISC_A:MISC_A + B_HEADS].set(dt_bias.astype(F32))
    row = lambda b, i: (b * n + i, 0)
    big = pl.BlockSpec((c_rows, width), row)
    small = pl.BlockSpec((8, width), row)
    sub = min(GDN_SUB, n)
    assert sub == GDN_SUB and n % sub == 0
    n2 = n // sub
    row2 = lambda b, i: (b * n2 + i, 0)
    big2 = pl.BlockSpec((sub * c_rows, width), row2)
    u, w, qd, kdt, attn, gl = pl.pallas_call(
        _gdn_prep_kernel,
        out_shape=[jax.ShapeDtypeStruct((m, width), F32)] * 5
        + [jax.ShapeDtypeStruct((bsz * n * 8, width), F32)],
        grid=(bsz, n2),
        in_specs=[pl.BlockSpec((sub * c_rows, 3 * width), lambda b, i: (b * n2 + i, OFF_BQKV // (3 * width))),
                  pl.BlockSpec((sub * c_rows, LANES), lambda b, i: (b * n2 + i, OFF_MISC // LANES)),
                  pl.BlockSpec((CONV_K, 3 * width), lambda b, i: (0, 0)),
                  pl.BlockSpec((8, LANES), lambda b, i: (0, 0))],
        out_specs=[big2] * 5 + [pl.BlockSpec((sub * 8, width), row2)],
        scratch_shapes=[pltpu.VMEM((sub * c_rows + 8, 3 * width), F32)],
        compiler_params=_cparams(("parallel", "arbitrary")),
        name="gdn_prep",
    )(proj, proj, conv_w, gp)
    return pl.pallas_call(
        _gdn_scan_kernel,
        out_shape=jax.ShapeDtypeStruct((m, width), F32),
        grid=(bsz, n),
        in_specs=[big] * 5 + [small,
                              pl.BlockSpec((c_rows, width), lambda b, i: (b * n + i, OFF_BZ // width)),
                              pl.BlockSpec((1, B_DIM), lambda b, i: (0, 0))],
        out_specs=big,
        scratch_shapes=[pltpu.VMEM((B_HEADS, B_DIM, B_DIM), F32)],
        compiler_params=_cparams(("parallel", "arbitrary")),
        name="gdn_scan",
    )(u, w, qd, kdt, attn, gl, proj, norm_g.reshape(1, B_DIM))


def _dsa_prep_kernel(cq_ref, ckv_ref, misc_ref, qg_ref, kvg_ref, kig_ref, kib_ref,
                     wuq_ref, wuk_ref, wqi_ref, qa_ref, qi_ref, ckvn_ref, kpad_ref, wi_ref):
    tm = cq_ref.shape[0]
    ql = (_rms(cq_ref[...]) * qg_ref[...]).astype(BF16)
    qh = _dot(ql, wuq_ref[...]).astype(BF16)
    qa = (_dot(qh, wuk_ref[...]) * (C_HEAD_DIM ** -0.5 * LOG2E)).astype(BF16)
    qi = _dot(ql, wqi_ref[...]).astype(BF16)
    for h in range(C_HEADS):
        qa_ref[h * tm:(h + 1) * tm, :] = qa[:, h * LANES:(h + 1) * LANES]
        qi_ref[h * tm:(h + 1) * tm, :] = qi[:, h * LANES:(h + 1) * LANES]
    ckvn_ref[...] = (_rms(ckv_ref[...]) * kvg_ref[...]).astype(BF16)
    mi = misc_ref[...]
    lane = lax.broadcasted_iota(I32, mi.shape, 1)
    isk = lane < IDX_DIM
    mu = jnp.sum(jnp.where(isk, mi, 0.0), axis=-1, keepdims=True) * (1.0 / IDX_DIM)
    xc = jnp.where(isk, mi - mu, 0.0)
    var = jnp.sum(xc * xc, axis=-1, keepdims=True) * (1.0 / IDX_DIM)
    kn = xc * lax.rsqrt(var + NORM_EPS) * kig_ref[...] + kib_ref[...]
    kpad_ref[...] = jnp.where(isk, kn, 0.0).astype(BF16)
    wi_ref[...] = mi * (IDX_HEADS ** -0.5 * IDX_DIM ** -0.5)


def _float_key(x):
    b = lax.bitcast_convert_type(x, I32)
    return jnp.where(b >= 0, b, b ^ INT_MAX)


def _bit_transpose32(words):
    w = list(words)
    j, m = 16, 0x0000FFFF
    while j:
        k = 0
        while k < 32:
            t = (w[k] ^ lax.shift_right_logical(w[k + j], j)) & m
            w[k] = w[k] ^ t
            w[k + j] = w[k + j] ^ lax.shift_left(t, j)
            k = (k + j + 1) & ~j
        j >>= 1
        m = (m ^ (m << j)) & 0x7FFFFFFF
    return w


def _dsa_main_kernel(qa_ref, qi_ref, wi_ref, kpad_ref, ckv_ref, wuv_ref, o_ref,
                     key_scr, alive_scr, gt_scr, sel_scr, m_scr, l_scr, acc_scr,
                     *, tq, tk, n_sel, idx_bits):
    i = pl.program_id(1)
    nvis = ((i + 1) * tq + tk - 1) // tk
    tiles = tk // LANES
    group = 32 // tiles
    span = 32 * LANES
    ngrp = (nvis + group - 1) // group
    rows = i * tq + lax.broadcasted_iota(I32, (tq, tk), 0)
    col0 = lax.broadcasted_iota(I32, (tq, tk), 1)
    lane = lax.broadcasted_iota(I32, (tq, LANES), 1)
    wi = wi_ref[...]

    def score_block(jb, carry):
        start = pl.multiple_of(jb * tk, tk)
        kblk = kpad_ref[pl.ds(start, tk), :]
        hc = IDX_HEADS // ATT_CHAINS
        ss = [_dot(qi_ref[ch * hc * tq:(ch + 1) * hc * tq, :], kblk, _NT) for ch in range(ATT_CHAINS)]
        acc = None
        for h in range(IDX_HEADS):
            s_h = ss[h // hc][(h % hc) * tq:(h % hc + 1) * tq]
            term = jnp.maximum(s_h, 0.0) * wi[:, MISC_WI + h:MISC_WI + h + 1]
            acc = term if acc is None else acc + term
        key = jnp.where(col0 + start <= rows, _float_key(acc) ^ INT_MIN, 0)
        for tt in range(tiles):
            key_scr[jb * tiles + tt] = key[:, tt * LANES:(tt + 1) * LANES]
        return carry

    _for_blocks(nvis, lambda jb: score_block(jb, 0))

    def fill_tile(kt, carry):
        key_scr[kt] = jnp.zeros((tq, LANES), I32)
        return carry

    lax.fori_loop(nvis * tiles, ngrp * 32, fill_tile, 0)

    def plane_group(g, carry):
        def plane_rows(rg, carry2):
            r0 = pl.multiple_of(rg * 8, 8)
            planes = _bit_transpose32([key_scr[g * 32 + k, pl.ds(r0, 8), :] for k in range(32)])
            for b in range(32):
                key_scr[g * 32 + b, pl.ds(r0, 8), :] = planes[b]
            return carry2
        lax.fori_loop(0, tq // 8, plane_rows, 0)
        alive_scr[g] = jnp.full((tq, LANES), -1, I32)
        gt_scr[g] = jnp.zeros((tq, LANES), I32)
        return carry

    lax.fori_loop(0, ngrp, plane_group, 0)

    def popcount_rows(mask_fn):
        def body(g, acc):
            return acc + lax.population_count(mask_fn(g))
        acc = lax.fori_loop(0, ngrp, body, jnp.zeros((tq, LANES), I32))
        return jnp.sum(acc.astype(F32), axis=-1, keepdims=True)

    def radix_step(it, need):
        ones = popcount_rows(lambda g: alive_scr[g] & key_scr[g * 32 + it])
        take = ones >= need
        flip = jnp.where(take, 0, -1)

        def update(g, c):
            a = alive_scr[g]
            p = key_scr[g * 32 + it]
            gt_scr[g] = gt_scr[g] | (a & p & flip)
            alive_scr[g] = a & (p ^ flip)
            return c
        lax.fori_loop(0, ngrp, update, 0)
        return jnp.where(take, need, need - ones)

    need = lax.fori_loop(0, 32, radix_step, jnp.full((tq, 1), float(n_sel), F32))
    ties = popcount_rows(lambda g: alive_scr[g])

    def cols_below(g, x):
        nk = jnp.clip((x - g * span - lane + (LANES - 1)) >> 7, 0, 32)
        top = ~lax.shift_right_logical(jnp.full((tq, LANES), -1, I32), jnp.minimum(nk, 31))
        return jnp.where(nk >= 32, -1, top)

    qpos = i * tq + lax.broadcasted_iota(I32, (tq, 1), 0)
    few = qpos < n_sel
    overflow = jnp.where(few, 0.0, ties - need)
    xcut_all = jnp.full((tq, 1), 1 << 30, I32)

    def resolve_ties():
        def cut_step(it, cut):
            cand = cut | lax.shift_left(jnp.int32(1), idx_bits - 1 - it)
            below = popcount_rows(lambda g: alive_scr[g] & cols_below(g, cand))
            return jnp.where(below < need, cand, cut)

        cut = lax.fori_loop(0, idx_bits, cut_step, jnp.zeros((tq, 1), I32))
        return jnp.where(overflow > 0.0, cut + 1, xcut_all)

    xcut = lax.cond(jnp.max(overflow) > 0.0, resolve_ties, lambda: xcut_all)

    def select_group(g, carry):
        chosen = gt_scr[g] | (alive_scr[g] & cols_below(g, xcut))
        sel_scr[g] = jnp.where(few, cols_below(g, qpos + 1), chosen)
        return carry

    lax.fori_loop(0, ngrp, select_group, 0)

    m_scr[...] = jnp.full(m_scr.shape, NEG, F32)
    l_scr[...] = jnp.zeros(l_scr.shape, F32)
    acc_scr[...] = jnp.zeros(acc_scr.shape, F32)

    def attend_block(jb, carry):
        start = pl.multiple_of(jb * tk, tk)
        sel = sel_scr[jb // group]
        k0 = (jb % group) * tiles
        bias = jnp.concatenate(
            [jnp.where(jnp.left_shift(sel, k0 + tt) < 0, 0.0, NEG) for tt in range(tiles)], axis=1)
        ckv = ckv_ref[pl.ds(start, tk), :]
        hc = C_HEADS // ATT_CHAINS
        rs = [slice(ch * hc * tq, (ch + 1) * hc * tq) for ch in range(ATT_CHAINS)]
        ss = [_dot(qa_ref[r_, :], ckv, _NT) for r_ in rs]
        ss = [(s.reshape(hc, tq, tk) + bias[None]).reshape(hc * tq, tk) for s in ss]
        m_prevs = [m_scr[r_, :] for r_ in rs]
        m_news = [jnp.maximum(mp, jnp.max(s, axis=-1, keepdims=True)) for mp, s in zip(m_prevs, ss)]
        alphas = [jnp.exp2(mp - mn) for mp, mn in zip(m_prevs, m_news)]
        ps = [jnp.exp2(s - jnp.tile(mn, (1, tiles))) for s, mn in zip(ss, m_news)]
        pvs = [_dot(p.astype(BF16), ckv) for p in ps]
        for r_, mn, al, p, pv in zip(rs, m_news, alphas, ps, pvs):
            l_scr[r_, :] = al * l_scr[r_, :] + jnp.sum(p, axis=-1, keepdims=True)
            m_scr[r_, :] = mn
            acc_scr[r_, :] = al * acc_scr[r_, :] + pv
        return carry

    _for_blocks(nvis, lambda jb: attend_block(jb, 0))
    o_all = acc_scr[...] / l_scr[...]
    o_lat = jnp.concatenate([o_all[h * tq:(h + 1) * tq] for h in range(C_HEADS)], axis=1)
    o_ref[...] = _dot(o_lat.astype(BF16), wuv_ref[...])


def _dsa_attention(proj, q_norm_g, kv_norm_g, kidx_g, kidx_b, w_uq, w_qidx, w_uk, w_uv,
                   bsz, t, tq=256, tk=512):
    m = bsz * t
    tq = min(tq, t)
    tk = min(tk, t)
    tm = tq
    hr = C_HEADS * C_KV_LORA
    wuq = w_uq.reshape(C_Q_LORA, C_HEADS * C_HEAD_DIM).astype(BF16)
    eye_h = jnp.eye(C_HEADS, dtype=F32)
    wuk_bd = jnp.einsum('rhd,hg->hdgr', w_uk, eye_h).reshape(C_HEADS * C_HEAD_DIM, hr).astype(BF16)
    wuv_bd = jnp.einsum('rhd,hg->hrgd', w_uv, eye_h).reshape(hr, C_HEADS * C_HEAD_DIM).astype(BF16)
    wqi = jnp.pad(w_qidx, ((0, 0), (0, 0), (0, LANES - IDX_DIM))).reshape(C_Q_LORA, IDX_HEADS * LANES).astype(BF16)
    pad_row = lambda v: jnp.pad(v.astype(F32), (0, LANES - IDX_DIM)).reshape(1, LANES)
    full = lambda shape: pl.BlockSpec(shape, lambda i: (0, 0))
    qa, qi, ckvn, kpad, wi = pl.pallas_call(
        _dsa_prep_kernel,
        out_shape=[jax.ShapeDtypeStruct((m * C_HEADS, LANES), BF16), jax.ShapeDtypeStruct((m * IDX_HEADS, LANES), BF16),
                   jax.ShapeDtypeStruct((m, LANES), BF16), jax.ShapeDtypeStruct((m, LANES), BF16),
                   jax.ShapeDtypeStruct((m, LANES), F32)],
        grid=(m // tm,),
        in_specs=[pl.BlockSpec((tm, C_Q_LORA), lambda i: (i, OFF_CQ // C_Q_LORA)),
                  pl.BlockSpec((tm, LANES), lambda i: (i, OFF_CKV // LANES)),
                  pl.BlockSpec((tm, LANES), lambda i: (i, OFF_MISC // LANES)),
                  full((1, C_Q_LORA)), full((1, LANES)), full((1, LANES)), full((1, LANES)),
                  full(wuq.shape), full(wuk_bd.shape), full(wqi.shape)],
        out_specs=[pl.BlockSpec((C_HEADS * tm, LANES), lambda i: (i, 0)),
                   pl.BlockSpec((IDX_HEADS * tm, LANES), lambda i: (i, 0)),
                   pl.BlockSpec((tm, LANES), lambda i: (i, 0)), pl.BlockSpec((tm, LANES), lambda i: (i, 0)),
                   pl.BlockSpec((tm, LANES), lambda i: (i, 0))],
        compiler_params=_cparams(("parallel",)),
        name="dsa_prep",
    )(proj, proj, proj, q_norm_g.reshape(1, C_Q_LORA), kv_norm_g.reshape(1, LANES),
      pad_row(kidx_g), pad_row(kidx_b), wuq, wuk_bd, wqi)

    nq = t // tq
    n_sel = min(INDEX_TOPK, t // 4)
    idx_bits = max(1, int(math.ceil(math.log2(t))))
    ngrp = -(-t // (32 * LANES))
    blk = lambda rows, w: pl.BlockSpec((rows, w), lambda b, i: (b * nq + i, 0))
    once = pl.Buffered(1)
    seq = lambda w: pl.BlockSpec((t, w), lambda b, i: (b, 0), pipeline_mode=once)
    return pl.pallas_call(
        functools.partial(_dsa_main_kernel, tq=tq, tk=tk, n_sel=n_sel, idx_bits=idx_bits),
        out_shape=jax.ShapeDtypeStruct((m, C_HEADS * C_HEAD_DIM), F32),
        grid=(bsz, nq),
        in_specs=[blk(C_HEADS * tq, LANES), blk(IDX_HEADS * tq, LANES), blk(tq, LANES), seq(LANES), seq(LANES),
                  pl.BlockSpec(wuv_bd.shape, lambda b, i: (0, 0), pipeline_mode=once)],
        out_specs=blk(tq, C_HEADS * C_HEAD_DIM),
        scratch_shapes=[pltpu.VMEM((ngrp * 32, tq, LANES), I32)]
        + [pltpu.VMEM((ngrp, tq, LANES), I32)] * 3
        + [pltpu.VMEM((C_HEADS * tq, LANES), F32)] * 3,
        compiler_params=_cparams(("parallel", "arbitrary")),
        name="dsa_main",
    )(qa, qi, wi, kpad, ckvn, wuv_bd)


def _merge_kernel(x_ref, g0_ref, g1_ref, g2_ref, ya_ref, yb_ref, yc_ref, bg_ref,
                  wa_ref, wb_ref, wc_ref, wo_ref, lg_ref, lb_ref, o_ref, *, alpha):
    bg = bg_ref[...]
    merged = None
    for idx, (g_ref, y_ref, w_ref) in enumerate(
            ((g0_ref, ya_ref, wa_ref), (g1_ref, yb_ref, wb_ref), (g2_ref, yc_ref, wc_ref))):
        gate = _sigmoid(g_ref[...] + bg[idx:idx + 1, :])
        term = gate * _dot(y_ref[...].astype(BF16), w_ref[...])
        merged = term if merged is None else merged + term
    y = alpha * x_ref[...] + _dot(merged.astype(BF16), wo_ref[...])
    o_ref[...] = _layer_norm(y, lg_ref[...], lb_ref[...])


def _merge(x, proj, ya, yb, yc, b_gate, wa, wb, wc, wo, ln_g, ln_b, alpha, tm=256):
    m, d = x.shape
    gate_blk = lambda k: pl.BlockSpec((tm, d), lambda i: (i, OFF_GATES // d + k))
    rowblk = lambda w: pl.BlockSpec((tm, w), lambda i: (i, 0))
    full = lambda a: pl.BlockSpec(a.shape, lambda i: (0, 0))
    bg = b_gate.reshape(3, d)
    ws = [wa.astype(BF16), wb.astype(BF16), wc.astype(BF16), wo.astype(BF16)]
    lg, lb = ln_g.reshape(1, d), ln_b.reshape(1, d)
    return pl.pallas_call(
        functools.partial(_merge_kernel, alpha=alpha),
        out_shape=jax.ShapeDtypeStruct((m, d), F32),
        grid=(m // tm,),
        in_specs=[rowblk(d), gate_blk(0), gate_blk(1), gate_blk(2),
                  rowblk(ya.shape[1]), rowblk(yb.shape[1]), rowblk(yc.shape[1]), full(bg)]
        + [full(w) for w in ws] + [full(lg), full(lb)],
        out_specs=rowblk(d),
        compiler_params=_cparams(("parallel",)),
        name="merge",
    )(x, proj, proj, proj, ya, yb, yc, bg, *ws, lg, lb)


def _ffn_kernel(x_ref, w1_ref, w2_ref, lg_ref, lb_ref, o_ref, acc_scr, *, alpha):
    k = pl.program_id(1)

    @pl.when(k == 0)
    def _():
        acc_scr[...] = jnp.zeros(acc_scr.shape, F32)

    h = jnp.maximum(_dot(x_ref[...].astype(BF16), w1_ref[...]), 0.0)
    acc_scr[...] += _dot((h * h).astype(BF16), w2_ref[...])

    @pl.when(k == pl.num_programs(1) - 1)
    def _():
        o_ref[...] = _layer_norm(alpha * x_ref[...] + acc_scr[...], lg_ref[...], lb_ref[...])


def _ffn(x, w1, w2, ln_g, ln_b, alpha, tm=512, tf=1024):
    m, d = x.shape
    dff = w1.shape[1]
    return pl.pallas_call(
        functools.partial(_ffn_kernel, alpha=alpha),
        out_shape=jax.ShapeDtypeStruct((m, d), F32),
        grid=(m // tm, dff // tf),
        in_specs=[pl.BlockSpec((tm, d), lambda i, k: (i, 0)),
                  pl.BlockSpec((d, tf), lambda i, k: (0, k)),
                  pl.BlockSpec((tf, d), lambda i, k: (k, 0)),
                  pl.BlockSpec((1, d), lambda i, k: (0, 0)),
                  pl.BlockSpec((1, d), lambda i, k: (0, 0))],
        out_specs=pl.BlockSpec((tm, d), lambda i, k: (i, 0)),
        scratch_shapes=[pltpu.VMEM((tm, d), F32)],
        compiler_params=_cparams(("parallel", "arbitrary")),
        name="ffn",
    )(x, w1.astype(BF16), w2.astype(BF16), ln_g.reshape(1, d), ln_b.reshape(1, d))


def kernel(x, w_in, b_gate, a_lambda, a_subln_g, b_conv_w, b_a_log, b_dt_bias, b_norm_g,
           c_q_norm_g, c_kv_norm_g, c_kidx_g, c_kidx_b, c_w_uq, c_w_qidx, c_w_uk, c_w_uv,
           w_branch_a, w_branch_b, w_branch_c, w_o, ln1_g, ln1_b, w_ff1, w_ff2, ln2_g, ln2_b):
    bsz, t, d = x.shape
    depth = w_in.shape[0]
    alpha = (2 * depth) ** 0.25
    xf = x.reshape(bsz * t, d)
    for l in range(depth):
        lam_init = 0.8 - 0.6 * math.exp(-0.3 * l)
        proj = _matmul(xf, _pack_w_in(w_in[l]), F32, tm=min(1024, bsz * t), tn=512)
        y_a = _diff_attention(proj, a_lambda[l], a_subln_g[l], bsz, t, lam_init)
        y_b = _gated_deltanet(proj, b_conv_w[l], b_a_log[l], b_dt_bias[l], b_norm_g[l], bsz, t)
        y_c = _dsa_attention(proj, c_q_norm_g[l], c_kv_norm_g[l], c_kidx_g[l], c_kidx_b[l],
                             c_w_uq[l], c_w_qidx[l], c_w_uk[l], c_w_uv[l], bsz, t)
        xf = _merge(xf, proj, y_a, y_b, y_c, b_gate[l], w_branch_a[l], w_branch_b[l], w_branch_c[l],
                    w_o[l], ln1_g[l], ln1_b[l], alpha)
        xf = _ffn(xf, w_ff1[l], w_ff2[l], ln2_g[l], ln2_b[l], alpha)
    return xf.reshape(bsz, t, d)
```

```python
import functools
import math

import jax
import jax.numpy as jnp
from jax import lax
from jax.experimental import pallas as pl
from jax.experimental.pallas import tpu as pltpu

F32 = jnp.float32
BF16 = jnp.bfloat16
I32 = jnp.int32

LANES = 128
NEG = -1e30
LOG2E = 1.4426950408889634
INT_MIN = -2 ** 31
INT_MAX = 2 ** 31 - 1
VMEM_LIMIT = 56 * 1024 * 1024

A_HEADS = 4
A_QK_DIM = 64
B_HEADS = 4
B_DIM = 128
CONV_K = 4
GDN_CHUNK = 128
GDN_SUB = 4
C_HEADS = 8
C_HEAD_DIM = 64
C_Q_LORA = 256
C_KV_LORA = 128
IDX_HEADS = 8
IDX_DIM = 64
INDEX_TOPK = 256
NORM_EPS = 1e-6
ATT_CHAINS = 2

OFF_AQ, OFF_AK, OFF_AV = 0, 512, 1024
OFF_BQKV, OFF_BZ = 1536, 3072
OFF_CQ, OFF_CKV, OFF_MISC, OFF_GATES = 3584, 3840, 3968, 4096
P_COLS = 7168
MISC_WI, MISC_A, MISC_B = 64, 72, 76

_NT = (((1,), (1,)), ((), ()))
_NN = (((1,), (0,)), ((), ()))


def _cparams(sem):
    return pltpu.CompilerParams(dimension_semantics=sem, vmem_limit_bytes=VMEM_LIMIT)


def _dot(a, b, dims=_NN):
    return lax.dot_general(a, b, dims, preferred_element_type=F32)


def _for_blocks(n, body):
    def pair(jj, carry):
        body(2 * jj)
        body(2 * jj + 1)
        return carry

    lax.fori_loop(0, n // 2, pair, 0)

    @pl.when(n % 2 == 1)
    def _():
        body(n - 1)


def _layer_norm(x, g, b):
    mu = jnp.mean(x, axis=-1, keepdims=True)
    xc = x - mu
    var = jnp.mean(xc * xc, axis=-1, keepdims=True)
    return xc * lax.rsqrt(var + NORM_EPS) * g + b


def _rms(x):
    return x * lax.rsqrt(jnp.mean(x * x, axis=-1, keepdims=True) + NORM_EPS)


def _sigmoid(x):
    return 1.0 / (1.0 + jnp.exp(-x))


def _mm_kernel(x_ref, w_ref, o_ref):
    o_ref[...] = _dot(x_ref[...].astype(BF16), w_ref[...]).astype(o_ref.dtype)


def _matmul(x, w, out_dtype, tm, tn):
    m, k = x.shape
    n = w.shape[1]
    return pl.pallas_call(
        _mm_kernel,
        out_shape=jax.ShapeDtypeStruct((m, n), out_dtype),
        grid=(m // tm, n // tn),
        in_specs=[pl.BlockSpec((tm, k), lambda i, j: (i, 0)),
                  pl.BlockSpec((k, tn), lambda i, j: (0, j))],
        out_specs=pl.BlockSpec((tm, tn), lambda i, j: (i, j)),
        compiler_params=_cparams(("parallel", "parallel")),
        name="in_proj",
    )(x, w)


def _pack_w_in(w):
    d = w.shape[0]
    misc = jnp.concatenate(
        [w[:, 3976:4040], w[:, 4040:4048], w[:, 3584:3588], w[:, 3588:3592],
         jnp.zeros((d, LANES - 80), w.dtype)], axis=1)
    packed = jnp.concatenate(
        [w[:, 0:3584], w[:, 3592:3848], w[:, 3848:3976], misc, w[:, 4048:7120]], axis=1)
    return packed.astype(BF16)


def _diffattn_kernel(lam_ref, g_ref, q_ref, k_ref, v_ref, o_ref, m_scr, l_scr, acc_scr, *, tq, tk, lam_init):
    i = pl.program_id(2)
    tiles = tk // LANES
    ratio = tq // tk
    q = q_ref[...] * (A_QK_DIM ** -0.5 * LOG2E)
    lane = lax.broadcasted_iota(I32, q.shape, 1)
    q2 = jnp.concatenate([jnp.where(lane < A_QK_DIM, q, 0.0),
                          jnp.where(lane >= A_QK_DIM, q, 0.0)], axis=0).astype(BF16)
    m_scr[...] = jnp.full(m_scr.shape, NEG, F32)
    l_scr[...] = jnp.zeros(l_scr.shape, F32)
    acc_scr[...] = jnp.zeros(acc_scr.shape, F32)

    def block(j, diagonal):
        start = pl.multiple_of(j * tk, tk)
        kb = k_ref[pl.ds(start, tk), :].astype(BF16)
        vb = v_ref[pl.ds(start, tk), :].astype(BF16)
        rs = [slice(0, tq), slice(tq, 2 * tq)]
        ss = [_dot(q2[r_], kb, _NT) for r_ in rs]
        if diagonal:
            r = lax.broadcasted_iota(I32, (tq, tk), 0)
            cc = lax.broadcasted_iota(I32, (tq, tk), 1)
            bias = jnp.where(cc + (j - i * ratio) * tk <= r, 0.0, NEG)
            ss = [s + bias for s in ss]
        m_prevs = [m_scr[r_, :] for r_ in rs]
        m_news = [jnp.maximum(mp, jnp.max(s, axis=-1, keepdims=True)) for mp, s in zip(m_prevs, ss)]
        alphas = [jnp.exp2(mp - mn) for mp, mn in zip(m_prevs, m_news)]
        ps = [jnp.exp2(s - jnp.tile(mn, (1, tiles))) for s, mn in zip(ss, m_news)]
        pvs = [_dot(p.astype(BF16), vb) for p in ps]
        for r_, mn, al, p, pv in zip(rs, m_news, alphas, ps, pvs):
            l_scr[r_, :] = al * l_scr[r_, :] + jnp.sum(p, axis=-1, keepdims=True)
            m_scr[r_, :] = mn
            acc_scr[r_, :] = al * acc_scr[r_, :] + pv

    _for_blocks(i * ratio, lambda j: block(j, False))
    for d in range(ratio):
        block(i * ratio + d, True)

    lp = lam_ref[...]
    s1 = jnp.sum(lp[0:1, :] * lp[1:2, :], axis=-1, keepdims=True)
    s2 = jnp.sum(lp[2:3, :] * lp[3:4, :], axis=-1, keepdims=True)
    lam = jnp.exp(s1) - jnp.exp(s2) + lam_init
    o_all = acc_scr[...] / l_scr[...]
    o = o_all[0:tq] - lam * o_all[tq:2 * tq]
    o_ref[...] = _rms(o) * g_ref[...] * (1.0 - lam_init)


def _diff_attention(proj, a_lambda, subln_g, bsz, t, lam_init, tq=512, tk=512):
    tq = min(tq, t)
    tk = min(tk, tq)
    nq = t // tq
    qo, ko, vo = OFF_AQ // LANES, OFF_AK // LANES, OFF_AV // LANES
    once = pl.Buffered(1)
    return pl.pallas_call(
        functools.partial(_diffattn_kernel, tq=tq, tk=tk, lam_init=lam_init),
        out_shape=jax.ShapeDtypeStruct((bsz * t, A_HEADS * LANES), F32),
        grid=(bsz, A_HEADS, nq),
        in_specs=[
            pl.BlockSpec((4, A_QK_DIM), lambda b, h, i: (0, 0)),
            pl.BlockSpec((1, LANES), lambda b, h, i: (0, 0)),
            pl.BlockSpec((tq, LANES), lambda b, h, i: (b * nq + i, qo + h)),
            pl.BlockSpec((t, LANES), lambda b, h, i: (b, ko + h), pipeline_mode=once),
            pl.BlockSpec((t, LANES), lambda b, h, i: (b, vo + h), pipeline_mode=once),
        ],
        out_specs=pl.BlockSpec((tq, LANES), lambda b, h, i: (b * nq + i, h)),
        scratch_shapes=[pltpu.VMEM((2 * tq, LANES), F32)] * 3,
        compiler_params=_cparams(("parallel", "parallel", "arbitrary")),
        name="diff_attention",
    )(a_lambda, subln_g.reshape(1, LANES), proj, proj, proj)


def _gdn_prep_kernel(qkv_ref, misc_ref, cw_ref, gp_ref,
                     u_ref, w_ref, qd_ref, kdt_ref, attn_ref, gl_ref, xs_scr):
    c_rows = GDN_CHUNK
    rows = c_rows * GDN_SUB
    n = pl.program_id(1)

    @pl.when(n == 0)
    def _():
        xs_scr[0:8, :] = jnp.zeros((8, xs_scr.shape[1]), F32)

    xs_scr[8:8 + rows, :] = qkv_ref[...]
    cw = cw_ref[...]
    y = cw[0:1, :] * xs_scr[5:5 + rows, :]
    for jj in range(1, CONV_K):
        y = y + cw[jj:jj + 1, :] * xs_scr[5 + jj:5 + jj + rows, :]
    xs_scr[0:8, :] = xs_scr[rows:rows + 8, :]
    y = y * _sigmoid(y)

    mi = misc_ref[...]
    gp = gp_ref[...]
    xg = mi + gp[1:2, :]
    softplus = jnp.maximum(xg, 0.0) + jnp.log(1.0 + jnp.exp(-jnp.abs(xg)))
    g_all = -jnp.exp(gp[0:1, :]) * softplus
    beta_all = _sigmoid(mi)

    r = lax.broadcasted_iota(I32, (c_rows, c_rows), 0)
    c = lax.broadcasted_iota(I32, (c_rows, c_rows), 1)
    incl = r >= c
    strict = r > c
    eye = (r == c).astype(F32)
    ones_l = incl.astype(BF16)
    g_hi = g_all.astype(BF16)
    g_r1 = g_all - g_hi.astype(F32)
    g_mid = g_r1.astype(BF16)
    g_lo = (g_r1 - g_mid.astype(F32)).astype(BF16)
    subs = [slice(sc * c_rows, (sc + 1) * c_rows) for sc in range(GDN_SUB)]
    gcs = [_dot(ones_l, g_hi[rs_]) + (_dot(ones_l, g_mid[rs_]) + _dot(ones_l, g_lo[rs_]))
           for rs_ in subs]
    gcts = [gc.T for gc in gcs]

    heads = range(GDN_SUB * B_HEADS)
    qs, ks, vs, gcols, bcols, decays, kbs, khbs = [], [], [], [], [], [], [], []
    for inst in heads:
        sc, h = divmod(inst, B_HEADS)
        rs_ = subs[sc]
        qh = y[rs_, h * B_DIM:(h + 1) * B_DIM]
        kh = y[rs_, 512 + h * B_DIM:512 + (h + 1) * B_DIM]
        qs.append(qh * lax.rsqrt(jnp.sum(qh * qh, axis=-1, keepdims=True) + NORM_EPS) * (B_DIM ** -0.5))
        kh = kh * lax.rsqrt(jnp.sum(kh * kh, axis=-1, keepdims=True) + NORM_EPS)
        ks.append(kh)
        vs.append(y[rs_, 1024 + h * B_DIM:1024 + (h + 1) * B_DIM])
        gcol = gcs[sc][:, MISC_A + h:MISC_A + h + 1]
        grow = gcts[sc][MISC_A + h:MISC_A + h + 1, :]
        gcols.append(gcol)
        bcols.append(beta_all[rs_, MISC_B + h:MISC_B + h + 1])
        decays.append(jnp.where(incl, jnp.exp(jnp.where(incl, gcol - grow, 0.0)), 0.0))
        kbs.append(kh * bcols[inst])
        khbs.append(kh.astype(BF16))
    lows = [jnp.where(strict, _dot(kbs[h].astype(BF16), khbs[h], _NT) * decays[h], 0.0) for h in heads]
    attns = [_dot(qs[h].astype(BF16), khbs[h], _NT) * decays[h] for h in heads]

    same16 = (r >> 4) == (c >> 4)
    pws = [jnp.where(same16, lows[h], 0.0) for h in heads]
    invs = [eye - pws[h] for h in heads]
    for _ in range(3):
        pwbs = [pws[h].astype(BF16) for h in heads]
        pws = [_dot(pwbs[h], pwbs[h]) for h in heads]
        invs = [invs[h] + _dot(invs[h].astype(BF16), pws[h].astype(BF16)) for h in heads]
    for sh in (4, 5, 6):
        pair = ((r >> (sh + 1)) == (c >> (sh + 1))) & ((r >> sh) != (c >> sh))
        invbs = [invs[h].astype(BF16) for h in heads]
        mids = [_dot(jnp.where(pair, lows[h], 0.0).astype(BF16), invbs[h]).astype(BF16) for h in heads]
        invs = [invs[h] - _dot(invbs[h], mids[h]) for h in heads]

    egs = [jnp.exp(gcols[h]) for h in heads]
    sols = [_dot(invs[h].astype(BF16),
                 jnp.concatenate([vs[h] * bcols[h], kbs[h] * egs[h]], axis=1).astype(BF16)) for h in heads]
    for inst in heads:
        sc, h = divmod(inst, B_HEADS)
        rs_ = subs[sc]
        sl = slice(h * B_DIM, (h + 1) * B_DIM)
        glast = gcs[sc][c_rows - 1:c_rows, MISC_A + h:MISC_A + h + 1]
        u_ref[rs_, sl] = sols[inst][:, :B_DIM]
        w_ref[rs_, sl] = sols[inst][:, B_DIM:]
        qd_ref[rs_, sl] = qs[inst] * egs[inst]
        kdt_ref[rs_, sl] = (ks[inst] * jnp.exp(glast - gcols[inst])).T
        attn_ref[rs_, sl] = attns[inst]
        gl_ref[sc * 8:(sc + 1) * 8, sl] = jnp.broadcast_to(jnp.exp(glast), (8, B_DIM))


def _gdn_scan_kernel(u_ref, w_ref, qd_ref, kdt_ref, attn_ref, gl_ref, z_ref, g_ref, o_ref, s_scr):
    n = pl.program_id(1)

    @pl.when(n == 0)
    def _():
        s_scr[...] = jnp.zeros(s_scr.shape, F32)

    g = g_ref[...]
    heads = range(B_HEADS)
    sls = [slice(h * B_DIM, (h + 1) * B_DIM) for h in heads]
    ss = [s_scr[h] for h in heads]
    for sc in range(GDN_SUB):
        rs_ = slice(sc * GDN_CHUNK, (sc + 1) * GDN_CHUNK)
        sbs = [ss[h].astype(BF16) for h in heads]
        wss = [_dot(w_ref[rs_, sls[h]].astype(BF16), sbs[h]) for h in heads]
        qss = [_dot(qd_ref[rs_, sls[h]].astype(BF16), sbs[h]) for h in heads]
        vbs = [(u_ref[rs_, sls[h]] - wss[h]).astype(BF16) for h in heads]
        os_ = [qss[h] + _dot(attn_ref[rs_, sls[h]].astype(BF16), vbs[h]) for h in heads]
        kvs = [_dot(kdt_ref[rs_, sls[h]].astype(BF16), vbs[h]) for h in heads]
        ss = [ss[h] * gl_ref[sc * 8:sc * 8 + 1, sls[h]] + kvs[h] for h in heads]
        for h in heads:
            z = z_ref[rs_, sls[h]]
            o_ref[rs_, sls[h]] = _rms(os_[h]) * g * (z * _sigmoid(z))
    for h in heads:
        s_scr[h] = ss[h]


def _gated_deltanet(proj, conv_w, a_log, dt_bias, norm_g, bsz, t):
    m = bsz * t
    c_rows = GDN_CHUNK
    n = t // c_rows
    width = B_HEADS * B_DIM
    gp = jnp.zeros((8, LANES), F32)
    gp = gp.at[0, MISC_A:MISC_A + B_HEADS].set(a_log.astype(F32))
    gp = gp.at[1, MISC_A:MISC_A + B_HEADS].set(dt_bias.astype(F32))
    sub = min(GDN_SUB, n)
    assert sub == GDN_SUB and n % sub == 0
    n2 = n // sub
    row2 = lambda b, i: (b * n2 + i, 0)
    big2 = pl.BlockSpec((sub * c_rows, width), row2)
    u, w, qd, kdt, attn, gl = pl.pallas_call(
        _gdn_prep_kernel,
        out_shape=[jax.ShapeDtypeStruct((m, width), F32)] * 5
        + [jax.ShapeDtypeStruct((bsz * n * 8, width), F32)],
        grid=(bsz, n2),
        in_specs=[pl.BlockSpec((sub * c_rows, 3 * width), lambda b, i: (b * n2 + i, OFF_BQKV // (3 * width))),
                  pl.BlockSpec((sub * c_rows, LANES), lambda b, i: (b * n2 + i, OFF_MISC // LANES)),
                  pl.BlockSpec((CONV_K, 3 * width), lambda b, i: (0, 0)),
                  pl.BlockSpec((8, LANES), lambda b, i: (0, 0))],
        out_specs=[big2] * 5 + [pl.BlockSpec((sub * 8, width), row2)],
        scratch_shapes=[pltpu.VMEM((sub * c_rows + 8, 3 * width), F32)],
        compiler_params=_cparams(("parallel", "arbitrary")),
        name="gdn_prep",
    )(proj, proj, conv_w, gp)
    return pl.pallas_call(
        _gdn_scan_kernel,
        out_shape=jax.ShapeDtypeStruct((m, width), F32),
        grid=(bsz, n2),
        in_specs=[big2] * 5 + [pl.BlockSpec((sub * 8, width), row2),
                               pl.BlockSpec((sub * c_rows, width), lambda b, i: (b * n2 + i, OFF_BZ // width)),
                               pl.BlockSpec((1, B_DIM), lambda b, i: (0, 0))],
        out_specs=big2,
        scratch_shapes=[pltpu.VMEM((B_HEADS, B_DIM, B_DIM), F32)],
        compiler_params=_cparams(("parallel", "arbitrary")),
        name="gdn_scan",
    )(u, w, qd, kdt, attn, gl, proj, norm_g.reshape(1, B_DIM))


def _dsa_prep_kernel(cq_ref, ckv_ref, misc_ref, qg_ref, kvg_ref, kig_ref, kib_ref,
                     wuq_ref, wuk_ref, wqi_ref, qa_ref, qi_ref, ckvn_ref, kpad_ref, wi_ref):
    tm = cq_ref.shape[0]
    ql = (_rms(cq_ref[...]) * qg_ref[...]).astype(BF16)
    qh = _dot(ql, wuq_ref[...]).astype(BF16)
    qa = (_dot(qh, wuk_ref[...]) * (C_HEAD_DIM ** -0.5 * LOG2E)).astype(BF16)
    qi = _dot(ql, wqi_ref[...]).astype(BF16)
    for h in range(C_HEADS):
        qa_ref[h * tm:(h + 1) * tm, :] = qa[:, h * LANES:(h + 1) * LANES]
        qi_ref[h * tm:(h + 1) * tm, :] = qi[:, h * LANES:(h + 1) * LANES]
    ckvn_ref[...] = (_rms(ckv_ref[...]) * kvg_ref[...]).astype(BF16)
    mi = misc_ref[...]
    lane = lax.broadcasted_iota(I32, mi.shape, 1)
    isk = lane < IDX_DIM
    mu = jnp.sum(jnp.where(isk, mi, 0.0), axis=-1, keepdims=True) * (1.0 / IDX_DIM)
    xc = jnp.where(isk, mi - mu, 0.0)
    var = jnp.sum(xc * xc, axis=-1, keepdims=True) * (1.0 / IDX_DIM)
    kn = xc * lax.rsqrt(var + NORM_EPS) * kig_ref[...] + kib_ref[...]
    kpad_ref[...] = jnp.where(isk, kn, 0.0).astype(BF16)
    wi_ref[...] = mi * (IDX_HEADS ** -0.5 * IDX_DIM ** -0.5)


def _float_key(x):
    b = lax.bitcast_convert_type(x, I32)
    return jnp.where(b >= 0, b, b ^ INT_MAX)


def _bit_transpose32(words):
    w = list(words)
    j, m = 16, 0x0000FFFF
    while j:
        k = 0
        while k < 32:
            t = (w[k] ^ lax.shift_right_logical(w[k + j], j)) & m
            w[k] = w[k] ^ t
            w[k + j] = w[k + j] ^ lax.shift_left(t, j)
            k = (k + j + 1) & ~j
        j >>= 1
        m = (m ^ (m << j)) & 0x7FFFFFFF
    return w


def _dsa_main_kernel(qa_ref, qi_ref, wi_ref, kpad_ref, ckv_ref, wuv_ref, o_ref,
                     key_scr, alive_scr, gt_scr, sel_scr, m_scr, l_scr, acc_scr,
                     *, tq, tk, n_sel, idx_bits):
    i = pl.program_id(1)
    nvis = ((i + 1) * tq + tk - 1) // tk
    tiles = tk // LANES
    group = 32 // tiles
    span = 32 * LANES
    ngrp = (nvis + group - 1) // group
    rows = i * tq + lax.broadcasted_iota(I32, (tq, tk), 0)
    col0 = lax.broadcasted_iota(I32, (tq, tk), 1)
    lane = lax.broadcasted_iota(I32, (tq, LANES), 1)
    wi = wi_ref[...]

    def score_block(jb, carry):
        start = pl.multiple_of(jb * tk, tk)
        kblk = kpad_ref[pl.ds(start, tk), :]
        hc = IDX_HEADS // ATT_CHAINS
        ss = [_dot(qi_ref[ch * hc * tq:(ch + 1) * hc * tq, :], kblk, _NT) for ch in range(ATT_CHAINS)]
        acc = None
        for h in range(IDX_HEADS):
            s_h = ss[h // hc][(h % hc) * tq:(h % hc + 1) * tq]
            term = jnp.maximum(s_h, 0.0) * wi[:, MISC_WI + h:MISC_WI + h + 1]
            acc = term if acc is None else acc + term
        key = jnp.where(col0 + start <= rows, _float_key(acc) ^ INT_MIN, 0)
        for tt in range(tiles):
            key_scr[jb * tiles + tt] = key[:, tt * LANES:(tt + 1) * LANES]
        return carry

    _for_blocks(nvis, lambda jb: score_block(jb, 0))

    def fill_tile(kt, carry):
        key_scr[kt] = jnp.zeros((tq, LANES), I32)
        return carry

    lax.fori_loop(nvis * tiles, ngrp * 32, fill_tile, 0)

    def plane_group(g, carry):
        def plane_rows(rg, carry2):
            r0 = pl.multiple_of(rg * 8, 8)
            planes = _bit_transpose32([key_scr[g * 32 + k, pl.ds(r0, 8), :] for k in range(32)])
            for b in range(32):
                key_scr[g * 32 + b, pl.ds(r0, 8), :] = planes[b]
            return carry2
        lax.fori_loop(0, tq // 8, plane_rows, 0)
        alive_scr[g] = jnp.full((tq, LANES), -1, I32)
        gt_scr[g] = jnp.zeros((tq, LANES), I32)
        return carry

    lax.fori_loop(0, ngrp, plane_group, 0)

    def popcount_rows(mask_fn):
        def body(g, acc):
            return acc + lax.population_count(mask_fn(g))
        acc = lax.fori_loop(0, ngrp, body, jnp.zeros((tq, LANES), I32))
        return jnp.sum(acc.astype(F32), axis=-1, keepdims=True)

    def radix_step(it, carry):
        need, ones = carry
        take = ones >= need
        flip = jnp.where(take, 0, -1)
        nxt = jnp.minimum(it + 1, 31)

        def update(g, acc):
            a = alive_scr[g]
            p = key_scr[g * 32 + it]
            gt_scr[g] = gt_scr[g] | (a & p & flip)
            a = a & (p ^ flip)
            alive_scr[g] = a
            return acc + lax.population_count(a & key_scr[g * 32 + nxt])
        acc = lax.fori_loop(0, ngrp, update, jnp.zeros((tq, LANES), I32))
        nxt_ones = jnp.sum(acc.astype(F32), axis=-1, keepdims=True)
        return jnp.where(take, need, need - ones), nxt_ones

    ones0 = popcount_rows(lambda g: key_scr[g * 32])
    need, _ = lax.fori_loop(0, 32, radix_step, (jnp.full((tq, 1), float(n_sel), F32), ones0))
    ties = popcount_rows(lambda g: alive_scr[g])

    def cols_below(g, x):
        nk = jnp.clip((x - g * span - lane + (LANES - 1)) >> 7, 0, 32)
        top = ~lax.shift_right_logical(jnp.full((tq, LANES), -1, I32), jnp.minimum(nk, 31))
        return jnp.where(nk >= 32, -1, top)

    qpos = i * tq + lax.broadcasted_iota(I32, (tq, 1), 0)
    few = qpos < n_sel
    overflow = jnp.where(few, 0.0, ties - need)
    xcut_all = jnp.full((tq, 1), 1 << 30, I32)

    def resolve_ties():
        def cut_step(it, cut):
            cand = cut | lax.shift_left(jnp.int32(1), idx_bits - 1 - it)
            below = popcount_rows(lambda g: alive_scr[g] & cols_below(g, cand))
            return jnp.where(below < need, cand, cut)

        cut = lax.fori_loop(0, idx_bits, cut_step, jnp.zeros((tq, 1), I32))
        return jnp.where(overflow > 0.0, cut + 1, xcut_all)

    xcut = lax.cond(jnp.max(overflow) > 0.0, resolve_ties, lambda: xcut_all)

    def select_group(g, carry):
        chosen = gt_scr[g] | (alive_scr[g] & cols_below(g, xcut))
        sel_scr[g] = jnp.where(few, cols_below(g, qpos + 1), chosen)
        return carry

    lax.fori_loop(0, ngrp, select_group, 0)

    m_scr[...] = jnp.full(m_scr.shape, NEG, F32)
    l_scr[...] = jnp.zeros(l_scr.shape, F32)
    acc_scr[...] = jnp.zeros(acc_scr.shape, F32)

    def attend_block(jb, carry):
        start = pl.multiple_of(jb * tk, tk)
        sel = sel_scr[jb // group]
        k0 = (jb % group) * tiles
        bias = jnp.concatenate(
            [jnp.where(jnp.left_shift(sel, k0 + tt) < 0, 0.0, NEG) for tt in range(tiles)], axis=1)
        ckv = ckv_ref[pl.ds(start, tk), :]
        hc = C_HEADS // ATT_CHAINS
        rs = [slice(ch * hc * tq, (ch + 1) * hc * tq) for ch in range(ATT_CHAINS)]
        ss = [_dot(qa_ref[r_, :], ckv, _NT) for r_ in rs]
        ss = [(s.reshape(hc, tq, tk) + bias[None]).reshape(hc * tq, tk) for s in ss]
        m_prevs = [m_scr[r_, :] for r_ in rs]
        m_news = [jnp.maximum(mp, jnp.max(s, axis=-1, keepdims=True)) for mp, s in zip(m_prevs, ss)]
        alphas = [jnp.exp2(mp - mn) for mp, mn in zip(m_prevs, m_news)]
        ps = [jnp.exp2(s - jnp.tile(mn, (1, tiles))) for s, mn in zip(ss, m_news)]
        pvs = [_dot(p.astype(BF16), ckv) for p in ps]
        for r_, mn, al, p, pv in zip(rs, m_news, alphas, ps, pvs):
            l_scr[r_, :] = al * l_scr[r_, :] + jnp.sum(p, axis=-1, keepdims=True)
            m_scr[r_, :] = mn
            acc_scr[r_, :] = al * acc_scr[r_, :] + pv
        return carry

    _for_blocks(nvis, lambda jb: attend_block(jb, 0))
    o_all = acc_scr[...] / l_scr[...]
    o_lat = jnp.concatenate([o_all[h * tq:(h + 1) * tq] for h in range(C_HEADS)], axis=1)
    o_ref[...] = _dot(o_lat.astype(BF16), wuv_ref[...])


def _dsa_attention(proj, q_norm_g, kv_norm_g, kidx_g, kidx_b, w_uq, w_qidx, w_uk, w_uv,
                   bsz, t, tq=256, tk=512):
    m = bsz * t
    tq = min(tq, t)
    tk = min(tk, t)
    tm = tq
    hr = C_HEADS * C_KV_LORA
    wuq = w_uq.reshape(C_Q_LORA, C_HEADS * C_HEAD_DIM).astype(BF16)
    eye_h = jnp.eye(C_HEADS, dtype=F32)
    wuk_bd = jnp.einsum('rhd,hg->hdgr', w_uk, eye_h).reshape(C_HEADS * C_HEAD_DIM, hr).astype(BF16)
    wuv_bd = jnp.einsum('rhd,hg->hrgd', w_uv, eye_h).reshape(hr, C_HEADS * C_HEAD_DIM).astype(BF16)
    wqi = jnp.pad(w_qidx, ((0, 0), (0, 0), (0, LANES - IDX_DIM))).reshape(C_Q_LORA, IDX_HEADS * LANES).astype(BF16)
    pad_row = lambda v: jnp.pad(v.astype(F32), (0, LANES - IDX_DIM)).reshape(1, LANES)
    full = lambda shape: pl.BlockSpec(shape, lambda i: (0, 0))
    qa, qi, ckvn, kpad, wi = pl.pallas_call(
        _dsa_prep_kernel,
        out_shape=[jax.ShapeDtypeStruct((m * C_HEADS, LANES), BF16), jax.ShapeDtypeStruct((m * IDX_HEADS, LANES), BF16),
                   jax.ShapeDtypeStruct((m, LANES), BF16), jax.ShapeDtypeStruct((m, LANES), BF16),
                   jax.ShapeDtypeStruct((m, LANES), F32)],
        grid=(m // tm,),
        in_specs=[pl.BlockSpec((tm, C_Q_LORA), lambda i: (i, OFF_CQ // C_Q_LORA)),
                  pl.BlockSpec((tm, LANES), lambda i: (i, OFF_CKV // LANES)),
                  pl.BlockSpec((tm, LANES), lambda i: (i, OFF_MISC // LANES)),
                  full((1, C_Q_LORA)), full((1, LANES)), full((1, LANES)), full((1, LANES)),
                  full(wuq.shape), full(wuk_bd.shape), full(wqi.shape)],
        out_specs=[pl.BlockSpec((C_HEADS * tm, LANES), lambda i: (i, 0)),
                   pl.BlockSpec((IDX_HEADS * tm, LANES), lambda i: (i, 0)),
                   pl.BlockSpec((tm, LANES), lambda i: (i, 0)), pl.BlockSpec((tm, LANES), lambda i: (i, 0)),
                   pl.BlockSpec((tm, LANES), lambda i: (i, 0))],
        compiler_params=_cparams(("parallel",)),
        name="dsa_prep",
    )(proj, proj, proj, q_norm_g.reshape(1, C_Q_LORA), kv_norm_g.reshape(1, LANES),
      pad_row(kidx_g), pad_row(kidx_b), wuq, wuk_bd, wqi)

    nq = t // tq
    n_sel = min(INDEX_TOPK, t // 4)
    idx_bits = max(1, int(math.ceil(math.log2(t))))
    ngrp = -(-t // (32 * LANES))
    blk = lambda rows, w: pl.BlockSpec((rows, w), lambda b, i: (b * nq + i, 0))
    once = pl.Buffered(1)
    seq = lambda w: pl.BlockSpec((t, w), lambda b, i: (b, 0), pipeline_mode=once)
    return pl.pallas_call(
        functools.partial(_dsa_main_kernel, tq=tq, tk=tk, n_sel=n_sel, idx_bits=idx_bits),
        out_shape=jax.ShapeDtypeStruct((m, C_HEADS * C_HEAD_DIM), F32),
        grid=(bsz, nq),
        in_specs=[blk(C_HEADS * tq, LANES), blk(IDX_HEADS * tq, LANES), blk(tq, LANES), seq(LANES), seq(LANES),
                  pl.BlockSpec(wuv_bd.shape, lambda b, i: (0, 0), pipeline_mode=once)],
        out_specs=blk(tq, C_HEADS * C_HEAD_DIM),
        scratch_shapes=[pltpu.VMEM((ngrp * 32, tq, LANES), I32)]
        + [pltpu.VMEM((ngrp, tq, LANES), I32)] * 3
        + [pltpu.VMEM((C_HEADS * tq, LANES), F32)] * 3,
        compiler_params=_cparams(("parallel", "arbitrary")),
        name="dsa_main",
    )(qa, qi, wi, kpad, ckvn, wuv_bd)


def _merge_kernel(x_ref, g0_ref, g1_ref, g2_ref, ya_ref, yb_ref, yc_ref, bg_ref,
                  wa_ref, wb_ref, wc_ref, wo_ref, lg_ref, lb_ref, o_ref, *, alpha):
    bg = bg_ref[...]
    merged = None
    for idx, (g_ref, y_ref, w_ref) in enumerate(
            ((g0_ref, ya_ref, wa_ref), (g1_ref, yb_ref, wb_ref), (g2_ref, yc_ref, wc_ref))):
        gate = _sigmoid(g_ref[...] + bg[idx:idx + 1, :])
        term = gate * _dot(y_ref[...].astype(BF16), w_ref[...])
        merged = term if merged is None else merged + term
    y = alpha * x_ref[...] + _dot(merged.astype(BF16), wo_ref[...])
    o_ref[...] = _layer_norm(y, lg_ref[...], lb_ref[...])


def _merge(x, proj, ya, yb, yc, b_gate, wa, wb, wc, wo, ln_g, ln_b, alpha, tm=512):
    m, d = x.shape
    tm = min(tm, m)
    gate_blk = lambda k: pl.BlockSpec((tm, d), lambda i: (i, OFF_GATES // d + k))
    rowblk = lambda w: pl.BlockSpec((tm, w), lambda i: (i, 0))
    full = lambda a: pl.BlockSpec(a.shape, lambda i: (0, 0))
    bg = b_gate.reshape(3, d)
    ws = [wa.astype(BF16), wb.astype(BF16), wc.astype(BF16), wo.astype(BF16)]
    lg, lb = ln_g.reshape(1, d), ln_b.reshape(1, d)
    return pl.pallas_call(
        functools.partial(_merge_kernel, alpha=alpha),
        out_shape=jax.ShapeDtypeStruct((m, d), F32),
        grid=(m // tm,),
        in_specs=[rowblk(d), gate_blk(0), gate_blk(1), gate_blk(2),
                  rowblk(ya.shape[1]), rowblk(yb.shape[1]), rowblk(yc.shape[1]), full(bg)]
        + [full(w) for w in ws] + [full(lg), full(lb)],
        out_specs=rowblk(d),
        compiler_params=_cparams(("parallel",)),
        name="merge",
    )(x, proj, proj, proj, ya, yb, yc, bg, *ws, lg, lb)


def _ffn_kernel(x_ref, w1_ref, w2_ref, lg_ref, lb_ref, o_ref, acc_scr, *, alpha):
    k = pl.program_id(1)

    @pl.when(k == 0)
    def _():
        acc_scr[...] = jnp.zeros(acc_scr.shape, F32)

    h = jnp.maximum(_dot(x_ref[...].astype(BF16), w1_ref[...]), 0.0)
    acc_scr[...] += _dot((h * h).astype(BF16), w2_ref[...])

    @pl.when(k == pl.num_programs(1) - 1)
    def _():
        o_ref[...] = _layer_norm(alpha * x_ref[...] + acc_scr[...], lg_ref[...], lb_ref[...])


def _ffn(x, w1, w2, ln_g, ln_b, alpha, tm=1024, tf=1024):
    m, d = x.shape
    tm = min(tm, m)
    dff = w1.shape[1]
    return pl.pallas_call(
        functools.partial(_ffn_kernel, alpha=alpha),
        out_shape=jax.ShapeDtypeStruct((m, d), F32),
        grid=(m // tm, dff // tf),
        in_specs=[pl.BlockSpec((tm, d), lambda i, k: (i, 0)),
                  pl.BlockSpec((d, tf), lambda i, k: (0, k)),
                  pl.BlockSpec((tf, d), lambda i, k: (k, 0)),
                  pl.BlockSpec((1, d), lambda i, k: (0, 0)),
                  pl.BlockSpec((1, d), lambda i, k: (0, 0))],
        out_specs=pl.BlockSpec((tm, d), lambda i, k: (i, 0)),
        scratch_shapes=[pltpu.VMEM((tm, d), F32)],
        compiler_params=_cparams(("parallel", "arbitrary")),
        name="ffn",
    )(x, w1.astype(BF16), w2.astype(BF16), ln_g.reshape(1, d), ln_b.reshape(1, d))


def kernel(x, w_in, b_gate, a_lambda, a_subln_g, b_conv_w, b_a_log, b_dt_bias, b_norm_g,
           c_q_norm_g, c_kv_norm_g, c_kidx_g, c_kidx_b, c_w_uq, c_w_qidx, c_w_uk, c_w_uv,
           w_branch_a, w_branch_b, w_branch_c, w_o, ln1_g, ln1_b, w_ff1, w_ff2, ln2_g, ln2_b):
    bsz, t, d = x.shape
    depth = w_in.shape[0]
    alpha = (2 * depth) ** 0.25
    xf = x.reshape(bsz * t, d)
    for l in range(depth):
        lam_init = 0.8 - 0.6 * math.exp(-0.3 * l)
        proj = _matmul(xf, _pack_w_in(w_in[l]), F32, tm=min(1024, bsz * t), tn=1024)
        y_a = _diff_attention(proj, a_lambda[l], a_subln_g[l], bsz, t, lam_init)
        y_b = _gated_deltanet(proj, b_conv_w[l], b_a_log[l], b_dt_bias[l], b_norm_g[l], bsz, t)
        y_c = _dsa_attention(proj, c_q_norm_g[l], c_kv_norm_g[l], c_kidx_g[l], c_kidx_b[l],
                             c_w_uq[l], c_w_qidx[l], c_w_uk[l], c_w_uv[l], bsz, t)
        xf = _merge(xf, proj, y_a, y_b, y_c, b_gate[l], w_branch_a[l], w_branch_b[l], w_branch_c[l],
                    w_o[l], ln1_g[l], ln1_b[l], alpha)
        xf = _ffn(xf, w_ff1[l], w_ff2[l], ln2_g[l], ln2_b[l], alpha)
    return xf.reshape(bsz, t, d)
```

```python
import functools
import math

import jax
import jax.numpy as jnp
from jax import lax
from jax.experimental import pallas as pl
from jax.experimental.pallas import tpu as pltpu

F32 = jnp.float32
BF16 = jnp.bfloat16
I32 = jnp.int32

LANES = 128
NEG = -1e30
LOG2E = 1.4426950408889634
INT_MIN = -2 ** 31
INT_MAX = 2 ** 31 - 1
VMEM_LIMIT = 56 * 1024 * 1024

A_HEADS = 4
A_QK_DIM = 64
B_HEADS = 4
B_DIM = 128
CONV_K = 4
GDN_CHUNK = 128
GDN_SUB = 4
C_HEADS = 8
C_HEAD_DIM = 64
C_Q_LORA = 256
C_KV_LORA = 128
IDX_HEADS = 8
IDX_DIM = 64
INDEX_TOPK = 256
NORM_EPS = 1e-6
RADIX_CHAINS = 2
ATT_CHAINS = 2

OFF_AQ, OFF_AK, OFF_AV = 0, 512, 1024
OFF_BQKV, OFF_BZ = 1536, 3072
OFF_CQ, OFF_CKV, OFF_MISC, OFF_GATES = 3584, 3840, 3968, 4096
P_COLS = 7168
MISC_WI, MISC_A, MISC_B = 64, 72, 76

_NT = (((1,), (1,)), ((), ()))
_NN = (((1,), (0,)), ((), ()))


def _cparams(sem):
    return pltpu.CompilerParams(dimension_semantics=sem, vmem_limit_bytes=VMEM_LIMIT)


def _dot(a, b, dims=_NN):
    return lax.dot_general(a, b, dims, preferred_element_type=F32)


def _for_blocks(n, body):
    def pair(jj, carry):
        body(2 * jj)
        body(2 * jj + 1)
        return carry

    lax.fori_loop(0, n // 2, pair, 0)

    @pl.when(n % 2 == 1)
    def _():
        body(n - 1)


def _layer_norm(x, g, b):
    mu = jnp.mean(x, axis=-1, keepdims=True)
    xc = x - mu
    var = jnp.mean(xc * xc, axis=-1, keepdims=True)
    return xc * lax.rsqrt(var + NORM_EPS) * g + b


def _rms(x):
    return x * lax.rsqrt(jnp.mean(x * x, axis=-1, keepdims=True) + NORM_EPS)


def _sigmoid(x):
    return 1.0 / (1.0 + jnp.exp(-x))


def _mm_kernel(x_ref, w_ref, o_ref):
    o_ref[...] = _dot(x_ref[...].astype(BF16), w_ref[...]).astype(o_ref.dtype)


def _matmul(x, w, out_dtype, tm, tn):
    m, k = x.shape
    n = w.shape[1]
    return pl.pallas_call(
        _mm_kernel,
        out_shape=jax.ShapeDtypeStruct((m, n), out_dtype),
        grid=(m // tm, n // tn),
        in_specs=[pl.BlockSpec((tm, k), lambda i, j: (i, 0)),
                  pl.BlockSpec((k, tn), lambda i, j: (0, j))],
        out_specs=pl.BlockSpec((tm, tn), lambda i, j: (i, j)),
        compiler_params=_cparams(("parallel", "parallel")),
        name="in_proj",
    )(x, w)


def _pack_w_in(w):
    d = w.shape[0]
    misc = jnp.concatenate(
        [w[:, 3976:4040], w[:, 4040:4048], w[:, 3584:3588], w[:, 3588:3592],
         jnp.zeros((d, LANES - 80), w.dtype)], axis=1)
    packed = jnp.concatenate(
        [w[:, 0:3584], w[:, 3592:3848], w[:, 3848:3976], misc, w[:, 4048:7120]], axis=1)
    return packed.astype(BF16)


def _diffattn_kernel(lam_ref, g_ref, q_ref, k_ref, v_ref, o_ref, m_scr, l_scr, acc_scr, *, tq, tk, lam_init):
    i = pl.program_id(2)
    tiles = tk // LANES
    ratio = tq // tk
    q = q_ref[...] * (A_QK_DIM ** -0.5 * LOG2E)
    lane = lax.broadcasted_iota(I32, q.shape, 1)
    q2 = jnp.concatenate([jnp.where(lane < A_QK_DIM, q, 0.0),
                          jnp.where(lane >= A_QK_DIM, q, 0.0)], axis=0).astype(BF16)
    m_scr[...] = jnp.full(m_scr.shape, NEG, F32)
    l_scr[...] = jnp.zeros(l_scr.shape, F32)
    acc_scr[...] = jnp.zeros(acc_scr.shape, F32)

    def block(j, diagonal):
        start = pl.multiple_of(j * tk, tk)
        kb = k_ref[pl.ds(start, tk), :].astype(BF16)
        vb = v_ref[pl.ds(start, tk), :].astype(BF16)
        rs = [slice(0, tq), slice(tq, 2 * tq)]
        ss = [_dot(q2[r_], kb, _NT) for r_ in rs]
        if diagonal:
            r = lax.broadcasted_iota(I32, (tq, tk), 0)
            cc = lax.broadcasted_iota(I32, (tq, tk), 1)
            bias = jnp.where(cc + (j - i * ratio) * tk <= r, 0.0, NEG)
            ss = [s + bias for s in ss]
        m_prevs = [m_scr[r_, :] for r_ in rs]
        m_news = [jnp.maximum(mp, jnp.max(s, axis=-1, keepdims=True)) for mp, s in zip(m_prevs, ss)]
        alphas = [jnp.exp2(mp - mn) for mp, mn in zip(m_prevs, m_news)]
        ps = [jnp.exp2(s - jnp.tile(mn, (1, tiles))) for s, mn in zip(ss, m_news)]
        pvs = [_dot(p.astype(BF16), vb) for p in ps]
        for r_, mn, al, p, pv in zip(rs, m_news, alphas, ps, pvs):
            l_scr[r_, :] = al * l_scr[r_, :] + jnp.sum(p, axis=-1, keepdims=True)
            m_scr[r_, :] = mn
            acc_scr[r_, :] = al * acc_scr[r_, :] + pv

    _for_blocks(i * ratio, lambda j: block(j, False))
    for d in range(ratio):
        block(i * ratio + d, True)

    lp = lam_ref[...]
    s1 = jnp.sum(lp[0:1, :] * lp[1:2, :], axis=-1, keepdims=True)
    s2 = jnp.sum(lp[2:3, :] * lp[3:4, :], axis=-1, keepdims=True)
    lam = jnp.exp(s1) - jnp.exp(s2) + lam_init
    o_all = acc_scr[...] / l_scr[...]
    o = o_all[0:tq] - lam * o_all[tq:2 * tq]
    o_ref[...] = _rms(o) * g_ref[...] * (1.0 - lam_init)


def _diff_attention(proj, a_lambda, subln_g, bsz, t, lam_init, tq=512, tk=512):
    tq = min(tq, t)
    tk = min(tk, tq)
    nq = t // tq
    qo, ko, vo = OFF_AQ // LANES, OFF_AK // LANES, OFF_AV // LANES
    once = pl.Buffered(1)
    return pl.pallas_call(
        functools.partial(_diffattn_kernel, tq=tq, tk=tk, lam_init=lam_init),
        out_shape=jax.ShapeDtypeStruct((bsz * t, A_HEADS * LANES), F32),
        grid=(bsz, A_HEADS, nq),
        in_specs=[
            pl.BlockSpec((4, A_QK_DIM), lambda b, h, i: (0, 0)),
            pl.BlockSpec((1, LANES), lambda b, h, i: (0, 0)),
            pl.BlockSpec((tq, LANES), lambda b, h, i: (b * nq + i, qo + h)),
            pl.BlockSpec((t, LANES), lambda b, h, i: (b, ko + h), pipeline_mode=once),
            pl.BlockSpec((t, LANES), lambda b, h, i: (b, vo + h), pipeline_mode=once),
        ],
        out_specs=pl.BlockSpec((tq, LANES), lambda b, h, i: (b * nq + i, h)),
        scratch_shapes=[pltpu.VMEM((2 * tq, LANES), F32)] * 3,
        compiler_params=_cparams(("parallel", "parallel", "arbitrary")),
        name="diff_attention",
    )(a_lambda, subln_g.reshape(1, LANES), proj, proj, proj)


def _gdn_prep_kernel(qkv_ref, misc_ref, cw_ref, gp_ref,
                     u_ref, w_ref, qd_ref, kdt_ref, attn_ref, gl_ref, xs_scr):
    c_rows = GDN_CHUNK
    rows = c_rows * GDN_SUB
    n = pl.program_id(1)

    @pl.when(n == 0)
    def _():
        xs_scr[0:8, :] = jnp.zeros((8, xs_scr.shape[1]), F32)

    xs_scr[8:8 + rows, :] = qkv_ref[...]
    cw = cw_ref[...]
    y = cw[0:1, :] * xs_scr[5:5 + rows, :]
    for jj in range(1, CONV_K):
        y = y + cw[jj:jj + 1, :] * xs_scr[5 + jj:5 + jj + rows, :]
    xs_scr[0:8, :] = xs_scr[rows:rows + 8, :]
    y = y * _sigmoid(y)

    mi = misc_ref[...]
    gp = gp_ref[...]
    xg = mi + gp[1:2, :]
    softplus = jnp.maximum(xg, 0.0) + jnp.log(1.0 + jnp.exp(-jnp.abs(xg)))
    g_all = -jnp.exp(gp[0:1, :]) * softplus
    beta_all = _sigmoid(mi)

    r = lax.broadcasted_iota(I32, (c_rows, c_rows), 0)
    c = lax.broadcasted_iota(I32, (c_rows, c_rows), 1)
    incl = r >= c
    strict = r > c
    eye = (r == c).astype(F32)
    ones_l = incl.astype(BF16)
    g_hi = g_all.astype(BF16)
    g_r1 = g_all - g_hi.astype(F32)
    g_mid = g_r1.astype(BF16)
    g_lo = (g_r1 - g_mid.astype(F32)).astype(BF16)
    subs = [slice(sc * c_rows, (sc + 1) * c_rows) for sc in range(GDN_SUB)]
    gcs = [_dot(ones_l, g_hi[rs_]) + (_dot(ones_l, g_mid[rs_]) + _dot(ones_l, g_lo[rs_]))
           for rs_ in subs]
    gcts = [gc.T for gc in gcs]

    heads = range(GDN_SUB * B_HEADS)
    qs, ks, vs, gcols, bcols, decays, kbs, khbs = [], [], [], [], [], [], [], []
    for inst in heads:
        sc, h = divmod(inst, B_HEADS)
        rs_ = subs[sc]
        qh = y[rs_, h * B_DIM:(h + 1) * B_DIM]
        kh = y[rs_, 512 + h * B_DIM:512 + (h + 1) * B_DIM]
        qs.append(qh * lax.rsqrt(jnp.sum(qh * qh, axis=-1, keepdims=True) + NORM_EPS) * (B_DIM ** -0.5))
        kh = kh * lax.rsqrt(jnp.sum(kh * kh, axis=-1, keepdims=True) + NORM_EPS)
        ks.append(kh)
        vs.append(y[rs_, 1024 + h * B_DIM:1024 + (h + 1) * B_DIM])
        gcol = gcs[sc][:, MISC_A + h:MISC_A + h + 1]
        grow = gcts[sc][MISC_A + h:MISC_A + h + 1, :]
        gcols.append(gcol)
        bcols.append(beta_all[rs_, MISC_B + h:MISC_B + h + 1])
        decays.append(jnp.where(incl, jnp.exp(jnp.where(incl, gcol - grow, 0.0)), 0.0))
        kbs.append(kh * bcols[inst])
        khbs.append(kh.astype(BF16))
    lows = [jnp.where(strict, _dot(kbs[h].astype(BF16), khbs[h], _NT) * decays[h], 0.0) for h in heads]
    attns = [_dot(qs[h].astype(BF16), khbs[h], _NT) * decays[h] for h in heads]

    same16 = (r >> 4) == (c >> 4)
    pws = [jnp.where(same16, lows[h], 0.0) for h in heads]
    invs = [eye - pws[h] for h in heads]
    for _ in range(3):
        pwbs = [pws[h].astype(BF16) for h in heads]
        pws = [_dot(pwbs[h], pwbs[h]) for h in heads]
        invs = [invs[h] + _dot(invs[h].astype(BF16), pws[h].astype(BF16)) for h in heads]
    for sh in (4, 5, 6):
        pair = ((r >> (sh + 1)) == (c >> (sh + 1))) & ((r >> sh) != (c >> sh))
        invbs = [invs[h].astype(BF16) for h in heads]
        mids = [_dot(jnp.where(pair, lows[h], 0.0).astype(BF16), invbs[h]).astype(BF16) for h in heads]
        invs = [invs[h] - _dot(invbs[h], mids[h]) for h in heads]

    egs = [jnp.exp(gcols[h]) for h in heads]
    sols = [_dot(invs[h].astype(BF16),
                 jnp.concatenate([vs[h] * bcols[h], kbs[h] * egs[h]], axis=1).astype(BF16)) for h in heads]
    for inst in heads:
        sc, h = divmod(inst, B_HEADS)
        rs_ = subs[sc]
        sl = slice(h * B_DIM, (h + 1) * B_DIM)
        glast = gcs[sc][c_rows - 1:c_rows, MISC_A + h:MISC_A + h + 1]
        u_ref[rs_, sl] = sols[inst][:, :B_DIM]
        w_ref[rs_, sl] = sols[inst][:, B_DIM:]
        qd_ref[rs_, sl] = qs[inst] * egs[inst]
        kdt_ref[rs_, sl] = (ks[inst] * jnp.exp(glast - gcols[inst])).T
        attn_ref[rs_, sl] = attns[inst]
        gl_ref[sc * 8:(sc + 1) * 8, sl] = jnp.broadcast_to(jnp.exp(glast), (8, B_DIM))


def _gdn_scan_kernel(u_ref, w_ref, qd_ref, kdt_ref, attn_ref, gl_ref, z_ref, g_ref, o_ref, s_scr):
    n = pl.program_id(1)

    @pl.when(n == 0)
    def _():
        s_scr[...] = jnp.zeros(s_scr.shape, F32)

    g = g_ref[...]
    heads = range(B_HEADS)
    sls = [slice(h * B_DIM, (h + 1) * B_DIM) for h in heads]
    ss = [s_scr[h] for h in heads]
    for sc in range(GDN_SUB):
        rs_ = slice(sc * GDN_CHUNK, (sc + 1) * GDN_CHUNK)
        sbs = [ss[h].astype(BF16) for h in heads]
        wss = [_dot(w_ref[rs_, sls[h]].astype(BF16), sbs[h]) for h in heads]
        qss = [_dot(qd_ref[rs_, sls[h]].astype(BF16), sbs[h]) for h in heads]
        vbs = [(u_ref[rs_, sls[h]] - wss[h]).astype(BF16) for h in heads]
        os_ = [qss[h] + _dot(attn_ref[rs_, sls[h]].astype(BF16), vbs[h]) for h in heads]
        kvs = [_dot(kdt_ref[rs_, sls[h]].astype(BF16), vbs[h]) for h in heads]
        ss = [ss[h] * gl_ref[sc * 8:sc * 8 + 1, sls[h]] + kvs[h] for h in heads]
        for h in heads:
            z = z_ref[rs_, sls[h]]
            o_ref[rs_, sls[h]] = _rms(os_[h]) * g * (z * _sigmoid(z))
    for h in heads:
        s_scr[h] = ss[h]


def _gated_deltanet(proj, conv_w, a_log, dt_bias, norm_g, bsz, t):
    m = bsz * t
    c_rows = GDN_CHUNK
    n = t // c_rows
    width = B_HEADS * B_DIM
    gp = jnp.zeros((8, LANES), F32)
    gp = gp.at[0, MISC_A:MISC_A + B_HEADS].set(a_log.astype(F32))
    gp = gp.at[1, MISC_A:MISC_A + B_HEADS].set(dt_bias.astype(F32))
    sub = min(GDN_SUB, n)
    assert sub == GDN_SUB and n % sub == 0
    n2 = n // sub
    row2 = lambda b, i: (b * n2 + i, 0)
    big2 = pl.BlockSpec((sub * c_rows, width), row2)
    u, w, qd, kdt, attn, gl = pl.pallas_call(
        _gdn_prep_kernel,
        out_shape=[jax.ShapeDtypeStruct((m, width), F32)] * 5
        + [jax.ShapeDtypeStruct((bsz * n * 8, width), F32)],
        grid=(bsz, n2),
        in_specs=[pl.BlockSpec((sub * c_rows, 3 * width), lambda b, i: (b * n2 + i, OFF_BQKV // (3 * width))),
                  pl.BlockSpec((sub * c_rows, LANES), lambda b, i: (b * n2 + i, OFF_MISC // LANES)),
                  pl.BlockSpec((CONV_K, 3 * width), lambda b, i: (0, 0)),
                  pl.BlockSpec((8, LANES), lambda b, i: (0, 0))],
        out_specs=[big2] * 5 + [pl.BlockSpec((sub * 8, width), row2)],
        scratch_shapes=[pltpu.VMEM((sub * c_rows + 8, 3 * width), F32)],
        compiler_params=_cparams(("parallel", "arbitrary")),
        name="gdn_prep",
    )(proj, proj, conv_w, gp)
    return pl.pallas_call(
        _gdn_scan_kernel,
        out_shape=jax.ShapeDtypeStruct((m, width), F32),
        grid=(bsz, n2),
        in_specs=[big2] * 5 + [pl.BlockSpec((sub * 8, width), row2),
                               pl.BlockSpec((sub * c_rows, width), lambda b, i: (b * n2 + i, OFF_BZ // width)),
                               pl.BlockSpec((1, B_DIM), lambda b, i: (0, 0))],
        out_specs=big2,
        scratch_shapes=[pltpu.VMEM((B_HEADS, B_DIM, B_DIM), F32)],
        compiler_params=_cparams(("parallel", "arbitrary")),
        name="gdn_scan",
    )(u, w, qd, kdt, attn, gl, proj, norm_g.reshape(1, B_DIM))


def _dsa_prep_kernel(cq_ref, ckv_ref, misc_ref, qg_ref, kvg_ref, kig_ref, kib_ref,
                     wuq_ref, wuk_ref, wqi_ref, qa_ref, qi_ref, ckvn_ref, kpad_ref, wi_ref):
    tm = cq_ref.shape[0]
    ql = (_rms(cq_ref[...]) * qg_ref[...]).astype(BF16)
    qh = _dot(ql, wuq_ref[...]).astype(BF16)
    qa = (_dot(qh, wuk_ref[...]) * (C_HEAD_DIM ** -0.5 * LOG2E)).astype(BF16)
    qi = _dot(ql, wqi_ref[...]).astype(BF16)
    for h in range(C_HEADS):
        qa_ref[h * tm:(h + 1) * tm, :] = qa[:, h * LANES:(h + 1) * LANES]
        qi_ref[h * tm:(h + 1) * tm, :] = qi[:, h * LANES:(h + 1) * LANES]
    ckvn_ref[...] = (_rms(ckv_ref[...]) * kvg_ref[...]).astype(BF16)
    mi = misc_ref[...]
    lane = lax.broadcasted_iota(I32, mi.shape, 1)
    isk = lane < IDX_DIM
    mu = jnp.sum(jnp.where(isk, mi, 0.0), axis=-1, keepdims=True) * (1.0 / IDX_DIM)
    xc = jnp.where(isk, mi - mu, 0.0)
    var = jnp.sum(xc * xc, axis=-1, keepdims=True) * (1.0 / IDX_DIM)
    kn = xc * lax.rsqrt(var + NORM_EPS) * kig_ref[...] + kib_ref[...]
    kpad_ref[...] = jnp.where(isk, kn, 0.0).astype(BF16)
    wi_ref[...] = mi * (IDX_HEADS ** -0.5 * IDX_DIM ** -0.5)


def _float_key(x):
    b = lax.bitcast_convert_type(x, I32)
    return jnp.where(b >= 0, b, b ^ INT_MAX)


def _bit_transpose32(words):
    w = list(words)
    j, m = 16, 0x0000FFFF
    while j:
        k = 0
        while k < 32:
            t = (w[k] ^ lax.shift_right_logical(w[k + j], j)) & m
            w[k] = w[k] ^ t
            w[k + j] = w[k + j] ^ lax.shift_left(t, j)
            k = (k + j + 1) & ~j
        j >>= 1
        m = (m ^ (m << j)) & 0x7FFFFFFF
    return w


def _dsa_main_kernel(qa_ref, qi_ref, wi_ref, kpad_ref, ckv_ref, wuv_ref, o_ref,
                     key_scr, alive_scr, gt_scr, sel_scr, m_scr, l_scr, acc_scr,
                     *, tq, tk, n_sel, idx_bits):
    i = pl.program_id(1)
    nvis = ((i + 1) * tq + tk - 1) // tk
    tiles = tk // LANES
    group = 32 // tiles
    span = 32 * LANES
    ngrp = (nvis + group - 1) // group
    rows = i * tq + lax.broadcasted_iota(I32, (tq, tk), 0)
    col0 = lax.broadcasted_iota(I32, (tq, tk), 1)
    lane = lax.broadcasted_iota(I32, (tq, LANES), 1)
    wi = wi_ref[...]

    def score_block(jb, carry):
        start = pl.multiple_of(jb * tk, tk)
        kblk = kpad_ref[pl.ds(start, tk), :]
        hc = IDX_HEADS // ATT_CHAINS
        ss = [_dot(qi_ref[ch * hc * tq:(ch + 1) * hc * tq, :], kblk, _NT) for ch in range(ATT_CHAINS)]
        acc = None
        for h in range(IDX_HEADS):
            s_h = ss[h // hc][(h % hc) * tq:(h % hc + 1) * tq]
            term = jnp.maximum(s_h, 0.0) * wi[:, MISC_WI + h:MISC_WI + h + 1]
            acc = term if acc is None else acc + term
        key = jnp.where(col0 + start <= rows, _float_key(acc) ^ INT_MIN, 0)
        for tt in range(tiles):
            key_scr[jb * tiles + tt] = key[:, tt * LANES:(tt + 1) * LANES]
        return carry

    _for_blocks(nvis, lambda jb: score_block(jb, 0))

    def fill_tile(kt, carry):
        key_scr[kt] = jnp.zeros((tq, LANES), I32)
        return carry

    lax.fori_loop(nvis * tiles, ngrp * 32, fill_tile, 0)

    def plane_group(g, carry):
        def plane_rows(rg, carry2):
            r0 = pl.multiple_of(rg * 8, 8)
            planes = _bit_transpose32([key_scr[g * 32 + k, pl.ds(r0, 8), :] for k in range(32)])
            for b in range(32):
                key_scr[g * 32 + b, pl.ds(r0, 8), :] = planes[b]
            return carry2
        lax.fori_loop(0, tq // 8, plane_rows, 0)
        alive_scr[g] = jnp.full((tq, LANES), -1, I32)
        gt_scr[g] = jnp.zeros((tq, LANES), I32)
        return carry

    lax.fori_loop(0, ngrp, plane_group, 0)

    def popcount_rows(mask_fn):
        def body(g, acc):
            return acc + lax.population_count(mask_fn(g))
        acc = lax.fori_loop(0, ngrp, body, jnp.zeros((tq, LANES), I32))
        return jnp.sum(acc.astype(F32), axis=-1, keepdims=True)

    chains = [slice(ch * tq // RADIX_CHAINS, (ch + 1) * tq // RADIX_CHAINS) for ch in range(RADIX_CHAINS)]

    def run_radix(ng):
        def radix_step(it, carry):
            needs, ones = carry
            takes = [o >= n_ for o, n_ in zip(ones, needs)]
            flips = [jnp.where(t_, 0, -1) for t_ in takes]
            nxt = jnp.minimum(it + 1, 31)
            accs = [jnp.zeros((tq // RADIX_CHAINS, LANES), I32) for _ in chains]
            for g in range(ng):
                for ch, r_ in enumerate(chains):
                    a = alive_scr[g, r_, :]
                    p = key_scr[g * 32 + it, r_, :]
                    gt_scr[g, r_, :] = gt_scr[g, r_, :] | (a & p & flips[ch])
                    a = a & (p ^ flips[ch])
                    alive_scr[g, r_, :] = a
                    accs[ch] = accs[ch] + lax.population_count(a & key_scr[g * 32 + nxt, r_, :])
            new_ones = tuple(jnp.sum(acc.astype(F32), axis=-1, keepdims=True) for acc in accs)
            new_needs = tuple(jnp.where(t_, n_, n_ - o) for t_, n_, o in zip(takes, needs, ones))
            return new_needs, new_ones

        def go():
            ones0 = []
            for r_ in chains:
                acc = lax.population_count(key_scr[0, r_, :])
                for g in range(1, ng):
                    acc = acc + lax.population_count(key_scr[g * 32, r_, :])
                ones0.append(jnp.sum(acc.astype(F32), axis=-1, keepdims=True))
            need0 = tuple(jnp.full((tq // RADIX_CHAINS, 1), float(n_sel), F32) for _ in chains)
            needs, _ = lax.fori_loop(0, 32, radix_step, (need0, tuple(ones0)))
            return jnp.concatenate(needs, axis=0)
        return go

    max_grp = key_scr.shape[0] // 32
    need = lax.switch(ngrp - 1, [run_radix(ng) for ng in range(1, max_grp + 1)])
    ties = popcount_rows(lambda g: alive_scr[g])

    def cols_below(g, x):
        nk = jnp.clip((x - g * span - lane + (LANES - 1)) >> 7, 0, 32)
        top = ~lax.shift_right_logical(jnp.full((tq, LANES), -1, I32), jnp.minimum(nk, 31))
        return jnp.where(nk >= 32, -1, top)

    qpos = i * tq + lax.broadcasted_iota(I32, (tq, 1), 0)
    few = qpos < n_sel
    overflow = jnp.where(few, 0.0, ties - need)
    xcut_all = jnp.full((tq, 1), 1 << 30, I32)

    def resolve_ties():
        def cut_step(it, cut):
            cand = cut | lax.shift_left(jnp.int32(1), idx_bits - 1 - it)
            below = popcount_rows(lambda g: alive_scr[g] & cols_below(g, cand))
            return jnp.where(below < need, cand, cut)

        cut = lax.fori_loop(0, idx_bits, cut_step, jnp.zeros((tq, 1), I32))
        return jnp.where(overflow > 0.0, cut + 1, xcut_all)

    xcut = lax.cond(jnp.max(overflow) > 0.0, resolve_ties, lambda: xcut_all)

    def select_group(g, carry):
        chosen = gt_scr[g] | (alive_scr[g] & cols_below(g, xcut))
        sel_scr[g] = jnp.where(few, cols_below(g, qpos + 1), chosen)
        return carry

    lax.fori_loop(0, ngrp, select_group, 0)

    m_scr[...] = jnp.full(m_scr.shape, NEG, F32)
    l_scr[...] = jnp.zeros(l_scr.shape, F32)
    acc_scr[...] = jnp.zeros(acc_scr.shape, F32)

    def attend_block(jb, carry):
        start = pl.multiple_of(jb * tk, tk)
        sel = sel_scr[jb // group]
        k0 = (jb % group) * tiles
        bias = jnp.concatenate(
            [jnp.where(jnp.left_shift(sel, k0 + tt) < 0, 0.0, NEG) for tt in range(tiles)], axis=1)
        ckv = ckv_ref[pl.ds(start, tk), :]
        hc = C_HEADS // ATT_CHAINS
        rs = [slice(ch * hc * tq, (ch + 1) * hc * tq) for ch in range(ATT_CHAINS)]
        ss = [_dot(qa_ref[r_, :], ckv, _NT) for r_ in rs]
        ss = [(s.reshape(hc, tq, tk) + bias[None]).reshape(hc * tq, tk) for s in ss]
        m_prevs = [m_scr[r_, :] for r_ in rs]
        m_news = [jnp.maximum(mp, jnp.max(s, axis=-1, keepdims=True)) for mp, s in zip(m_prevs, ss)]
        alphas = [jnp.exp2(mp - mn) for mp, mn in zip(m_prevs, m_news)]
        ps = [jnp.exp2(s - jnp.tile(mn, (1, tiles))) for s, mn in zip(ss, m_news)]
        pvs = [_dot(p.astype(BF16), ckv) for p in ps]
        for r_, mn, al, p, pv in zip(rs, m_news, alphas, ps, pvs):
            l_scr[r_, :] = al * l_scr[r_, :] + jnp.sum(p, axis=-1, keepdims=True)
            m_scr[r_, :] = mn
            acc_scr[r_, :] = al * acc_scr[r_, :] + pv
        return carry

    _for_blocks(nvis, lambda jb: attend_block(jb, 0))
    o_all = acc_scr[...] / l_scr[...]
    o_lat = jnp.concatenate([o_all[h * tq:(h + 1) * tq] for h in range(C_HEADS)], axis=1)
    o_ref[...] = _dot(o_lat.astype(BF16), wuv_ref[...])


def _dsa_attention(proj, q_norm_g, kv_norm_g, kidx_g, kidx_b, w_uq, w_qidx, w_uk, w_uv,
                   bsz, t, tq=256, tk=512):
    m = bsz * t
    tq = min(tq, t)
    tk = min(tk, t)
    tm = tq
    hr = C_HEADS * C_KV_LORA
    wuq = w_uq.reshape(C_Q_LORA, C_HEADS * C_HEAD_DIM).astype(BF16)
    eye_h = jnp.eye(C_HEADS, dtype=F32)
    wuk_bd = jnp.einsum('rhd,hg->hdgr', w_uk, eye_h).reshape(C_HEADS * C_HEAD_DIM, hr).astype(BF16)
    wuv_bd = jnp.einsum('rhd,hg->hrgd', w_uv, eye_h).reshape(hr, C_HEADS * C_HEAD_DIM).astype(BF16)
    wqi = jnp.pad(w_qidx, ((0, 0), (0, 0), (0, LANES - IDX_DIM))).reshape(C_Q_LORA, IDX_HEADS * LANES).astype(BF16)
    pad_row = lambda v: jnp.pad(v.astype(F32), (0, LANES - IDX_DIM)).reshape(1, LANES)
    full = lambda shape: pl.BlockSpec(shape, lambda i: (0, 0))
    qa, qi, ckvn, kpad, wi = pl.pallas_call(
        _dsa_prep_kernel,
        out_shape=[jax.ShapeDtypeStruct((m * C_HEADS, LANES), BF16), jax.ShapeDtypeStruct((m * IDX_HEADS, LANES), BF16),
                   jax.ShapeDtypeStruct((m, LANES), BF16), jax.ShapeDtypeStruct((m, LANES), BF16),
                   jax.ShapeDtypeStruct((m, LANES), F32)],
        grid=(m // tm,),
        in_specs=[pl.BlockSpec((tm, C_Q_LORA), lambda i: (i, OFF_CQ // C_Q_LORA)),
                  pl.BlockSpec((tm, LANES), lambda i: (i, OFF_CKV // LANES)),
                  pl.BlockSpec((tm, LANES), lambda i: (i, OFF_MISC // LANES)),
                  full((1, C_Q_LORA)), full((1, LANES)), full((1, LANES)), full((1, LANES)),
                  full(wuq.shape), full(wuk_bd.shape), full(wqi.shape)],
        out_specs=[pl.BlockSpec((C_HEADS * tm, LANES), lambda i: (i, 0)),
                   pl.BlockSpec((IDX_HEADS * tm, LANES), lambda i: (i, 0)),
                   pl.BlockSpec((tm, LANES), lambda i: (i, 0)), pl.BlockSpec((tm, LANES), lambda i: (i, 0)),
                   pl.BlockSpec((tm, LANES), lambda i: (i, 0))],
        compiler_params=_cparams(("parallel",)),
        name="dsa_prep",
    )(proj, proj, proj, q_norm_g.reshape(1, C_Q_LORA), kv_norm_g.reshape(1, LANES),
      pad_row(kidx_g), pad_row(kidx_b), wuq, wuk_bd, wqi)

    nq = t // tq
    n_sel = min(INDEX_TOPK, t // 4)
    idx_bits = max(1, int(math.ceil(math.log2(t))))
    ngrp = -(-t // (32 * LANES))
    blk = lambda rows, w: pl.BlockSpec((rows, w), lambda b, i: (b * nq + i, 0))
    once = pl.Buffered(1)
    seq = lambda w: pl.BlockSpec((t, w), lambda b, i: (b, 0), pipeline_mode=once)
    return pl.pallas_call(
        functools.partial(_dsa_main_kernel, tq=tq, tk=tk, n_sel=n_sel, idx_bits=idx_bits),
        out_shape=jax.ShapeDtypeStruct((m, C_HEADS * C_HEAD_DIM), F32),
        grid=(bsz, nq),
        in_specs=[blk(C_HEADS * tq, LANES), blk(IDX_HEADS * tq, LANES), blk(tq, LANES), seq(LANES), seq(LANES),
                  pl.BlockSpec(wuv_bd.shape, lambda b, i: (0, 0), pipeline_mode=once)],
        out_specs=blk(tq, C_HEADS * C_HEAD_DIM),
        scratch_shapes=[pltpu.VMEM((ngrp * 32, tq, LANES), I32)]
        + [pltpu.VMEM((ngrp, tq, LANES), I32)] * 3
        + [pltpu.VMEM((C_HEADS * tq, LANES), F32)] * 3,
        compiler_params=_cparams(("parallel", "arbitrary")),
        name="dsa_main",
    )(qa, qi, wi, kpad, ckvn, wuv_bd)


def _merge_kernel(x_ref, g0_ref, g1_ref, g2_ref, ya_ref, yb_ref, yc_ref, bg_ref,
                  wa_ref, wb_ref, wc_ref, wo_ref, lg_ref, lb_ref, o_ref, *, alpha):
    bg = bg_ref[...]
    merged = None
    for idx, (g_ref, y_ref, w_ref) in enumerate(
            ((g0_ref, ya_ref, wa_ref), (g1_ref, yb_ref, wb_ref), (g2_ref, yc_ref, wc_ref))):
        gate = _sigmoid(g_ref[...] + bg[idx:idx + 1, :])
        term = gate * _dot(y_ref[...].astype(BF16), w_ref[...])
        merged = term if merged is None else merged + term
    y = alpha * x_ref[...] + _dot(merged.astype(BF16), wo_ref[...])
    o_ref[...] = _layer_norm(y, lg_ref[...], lb_ref[...])


def _merge(x, proj, ya, yb, yc, b_gate, wa, wb, wc, wo, ln_g, ln_b, alpha, tm=512):
    m, d = x.shape
    tm = min(tm, m)
    gate_blk = lambda k: pl.BlockSpec((tm, d), lambda i: (i, OFF_GATES // d + k))
    rowblk = lambda w: pl.BlockSpec((tm, w), lambda i: (i, 0))
    full = lambda a: pl.BlockSpec(a.shape, lambda i: (0, 0))
    bg = b_gate.reshape(3, d)
    ws = [wa.astype(BF16), wb.astype(BF16), wc.astype(BF16), wo.astype(BF16)]
    lg, lb = ln_g.reshape(1, d), ln_b.reshape(1, d)
    return pl.pallas_call(
        functools.partial(_merge_kernel, alpha=alpha),
        out_shape=jax.ShapeDtypeStruct((m, d), F32),
        grid=(m // tm,),
        in_specs=[rowblk(d), gate_blk(0), gate_blk(1), gate_blk(2),
                  rowblk(ya.shape[1]), rowblk(yb.shape[1]), rowblk(yc.shape[1]), full(bg)]
        + [full(w) for w in ws] + [full(lg), full(lb)],
        out_specs=rowblk(d),
        compiler_params=_cparams(("parallel",)),
        name="merge",
    )(x, proj, proj, proj, ya, yb, yc, bg, *ws, lg, lb)


def _ffn_kernel(x_ref, w1_ref, w2_ref, lg_ref, lb_ref, o_ref, acc_scr, *, alpha):
    k = pl.program_id(1)

    @pl.when(k == 0)
    def _():
        acc_scr[...] = jnp.zeros(acc_scr.shape, F32)

    h = jnp.maximum(_dot(x_ref[...].astype(BF16), w1_ref[...]), 0.0)
    acc_scr[...] += _dot((h * h).astype(BF16), w2_ref[...])

    @pl.when(k == pl.num_programs(1) - 1)
    def _():
        o_ref[...] = _layer_norm(alpha * x_ref[...] + acc_scr[...], lg_ref[...], lb_ref[...])


def _ffn(x, w1, w2, ln_g, ln_b, alpha, tm=1024, tf=1024):
    m, d = x.shape
    tm = min(tm, m)
    dff = w1.shape[1]
    return pl.pallas_call(
        functools.partial(_ffn_kernel, alpha=alpha),
        out_shape=jax.ShapeDtypeStruct((m, d), F32),
        grid=(m // tm, dff // tf),
        in_specs=[pl.BlockSpec((tm, d), lambda i, k: (i, 0)),
                  pl.BlockSpec((d, tf), lambda i, k: (0, k)),
                  pl.BlockSpec((tf, d), lambda i, k: (k, 0)),
                  pl.BlockSpec((1, d), lambda i, k: (0, 0)),
                  pl.BlockSpec((1, d), lambda i, k: (0, 0))],
        out_specs=pl.BlockSpec((tm, d), lambda i, k: (i, 0)),
        scratch_shapes=[pltpu.VMEM((tm, d), F32)],
        compiler_params=_cparams(("parallel", "arbitrary")),
        name="ffn",
    )(x, w1.astype(BF16), w2.astype(BF16), ln_g.reshape(1, d), ln_b.reshape(1, d))


def kernel(x, w_in, b_gate, a_lambda, a_subln_g, b_conv_w, b_a_log, b_dt_bias, b_norm_g,
           c_q_norm_g, c_kv_norm_g, c_kidx_g, c_kidx_b, c_w_uq, c_w_qidx, c_w_uk, c_w_uv,
           w_branch_a, w_branch_b, w_branch_c, w_o, ln1_g, ln1_b, w_ff1, w_ff2, ln2_g, ln2_b):
    bsz, t, d = x.shape
    depth = w_in.shape[0]
    alpha = (2 * depth) ** 0.25
    xf = x.reshape(bsz * t, d)
    for l in range(depth):
        lam_init = 0.8 - 0.6 * math.exp(-0.3 * l)
        proj = _matmul(xf, _pack_w_in(w_in[l]), F32, tm=min(1024, bsz * t), tn=1024)
        y_a = _diff_attention(proj, a_lambda[l], a_subln_g[l], bsz, t, lam_init)
        y_b = _gated_deltanet(proj, b_conv_w[l], b_a_log[l], b_dt_bias[l], b_norm_g[l], bsz, t)
        y_c = _dsa_attention(proj, c_q_norm_g[l], c_kv_norm_g[l], c_kidx_g[l], c_kidx_b[l],
                             c_w_uq[l], c_w_qidx[l], c_w_uk[l], c_w_uv[l], bsz, t)
        xf = _merge(xf, proj, y_a, y_b, y_c, b_gate[l], w_branch_a[l], w_branch_b[l], w_branch_c[l],
                    w_o[l], ln1_g[l], ln1_b[l], alpha)
        xf = _ffn(xf, w_ff1[l], w_ff2[l], ln2_g[l], ln2_b[l], alpha)
    return xf.reshape(bsz, t, d)
```

```python
import functools
import math

import jax
import jax.numpy as jnp
from jax import lax
from jax.experimental import pallas as pl
from jax.experimental.pallas import tpu as pltpu

F32 = jnp.float32
BF16 = jnp.bfloat16
I32 = jnp.int32

LANES = 128
NEG = -1e30
LOG2E = 1.4426950408889634
INT_MIN = -2 ** 31
INT_MAX = 2 ** 31 - 1
VMEM_LIMIT = 56 * 1024 * 1024

A_HEADS = 4
A_QK_DIM = 64
B_HEADS = 4
B_DIM = 128
CONV_K = 4
GDN_CHUNK = 128
GDN_SUB = 4
C_HEADS = 8
C_HEAD_DIM = 64
C_Q_LORA = 256
C_KV_LORA = 128
IDX_HEADS = 8
IDX_DIM = 64
INDEX_TOPK = 256
NORM_EPS = 1e-6
RADIX_CHAINS = 2
ATT_CHAINS = 2

OFF_AQ, OFF_AK, OFF_AV = 0, 512, 1024
OFF_BQKV, OFF_BZ = 1536, 3072
OFF_CQ, OFF_CKV, OFF_MISC, OFF_GATES = 3584, 3840, 3968, 4096
P_COLS = 7168
MISC_WI, MISC_A, MISC_B = 64, 72, 76

_NT = (((1,), (1,)), ((), ()))
_NN = (((1,), (0,)), ((), ()))


def _cparams(sem):
    return pltpu.CompilerParams(dimension_semantics=sem, vmem_limit_bytes=VMEM_LIMIT)


def _dot(a, b, dims=_NN):
    return lax.dot_general(a, b, dims, preferred_element_type=F32)


def _for_blocks(n, body, per_trip=2):
    def trip(jj, carry):
        for u in range(per_trip):
            body(per_trip * jj + u)
        return carry

    lax.fori_loop(0, n // per_trip, trip, 0)
    done = (n // per_trip) * per_trip
    step = per_trip // 2
    while step:
        has = (n - done) >= step

        @pl.when(has)
        def _(done=done, step=step):
            for u in range(step):
                body(done + u)
        done = done + jnp.where(has, step, 0)
        step //= 2


def _layer_norm(x, g, b):
    mu = jnp.mean(x, axis=-1, keepdims=True)
    xc = x - mu
    var = jnp.mean(xc * xc, axis=-1, keepdims=True)
    return xc * lax.rsqrt(var + NORM_EPS) * g + b


def _rms(x):
    return x * lax.rsqrt(jnp.mean(x * x, axis=-1, keepdims=True) + NORM_EPS)


def _sigmoid(x):
    return 1.0 / (1.0 + jnp.exp(-x))


def _mm_kernel(x_ref, w_ref, o_ref):
    o_ref[...] = _dot(x_ref[...].astype(BF16), w_ref[...]).astype(o_ref.dtype)


def _matmul(x, w, out_dtype, tm, tn):
    m, k = x.shape
    n = w.shape[1]
    return pl.pallas_call(
        _mm_kernel,
        out_shape=jax.ShapeDtypeStruct((m, n), out_dtype),
        grid=(m // tm, n // tn),
        in_specs=[pl.BlockSpec((tm, k), lambda i, j: (i, 0)),
                  pl.BlockSpec((k, tn), lambda i, j: (0, j))],
        out_specs=pl.BlockSpec((tm, tn), lambda i, j: (i, j)),
        compiler_params=_cparams(("parallel", "parallel")),
        name="in_proj",
    )(x, w)


def _pack_w_in(w):
    d = w.shape[0]
    misc = jnp.concatenate(
        [w[:, 3976:4040], w[:, 4040:4048], w[:, 3584:3588], w[:, 3588:3592],
         jnp.zeros((d, LANES - 80), w.dtype)], axis=1)
    packed = jnp.concatenate(
        [w[:, 0:3584], w[:, 3592:3848], w[:, 3848:3976], misc, w[:, 4048:7120]], axis=1)
    return packed.astype(BF16)


def _diffattn_kernel(lam_ref, g_ref, q_ref, k_ref, v_ref, o_ref, m_scr, l_scr, acc_scr, *, tq, tk, lam_init):
    i = pl.program_id(2)
    tiles = tk // LANES
    ratio = tq // tk
    q = q_ref[...] * (A_QK_DIM ** -0.5 * LOG2E)
    lane = lax.broadcasted_iota(I32, q.shape, 1)
    q2 = jnp.concatenate([jnp.where(lane < A_QK_DIM, q, 0.0),
                          jnp.where(lane >= A_QK_DIM, q, 0.0)], axis=0).astype(BF16)
    m_scr[...] = jnp.full(m_scr.shape, NEG, F32)
    l_scr[...] = jnp.zeros(l_scr.shape, F32)
    acc_scr[...] = jnp.zeros(acc_scr.shape, F32)

    def block(j, diagonal):
        start = pl.multiple_of(j * tk, tk)
        kb = k_ref[pl.ds(start, tk), :].astype(BF16)
        vb = v_ref[pl.ds(start, tk), :].astype(BF16)
        rs = [slice(0, tq), slice(tq, 2 * tq)]
        ss = [_dot(q2[r_], kb, _NT) for r_ in rs]
        if diagonal:
            r = lax.broadcasted_iota(I32, (tq, tk), 0)
            cc = lax.broadcasted_iota(I32, (tq, tk), 1)
            bias = jnp.where(cc + (j - i * ratio) * tk <= r, 0.0, NEG)
            ss = [s + bias for s in ss]
        m_prevs = [m_scr[r_, :] for r_ in rs]
        m_news = [jnp.maximum(mp, jnp.max(s, axis=-1, keepdims=True)) for mp, s in zip(m_prevs, ss)]
        alphas = [jnp.exp2(mp - mn) for mp, mn in zip(m_prevs, m_news)]
        ps = [jnp.exp2(s - jnp.tile(mn, (1, tiles))) for s, mn in zip(ss, m_news)]
        pvs = [_dot(p.astype(BF16), vb) for p in ps]
        for r_, mn, al, p, pv in zip(rs, m_news, alphas, ps, pvs):
            l_scr[r_, :] = al * l_scr[r_, :] + jnp.sum(p, axis=-1, keepdims=True)
            m_scr[r_, :] = mn
            acc_scr[r_, :] = al * acc_scr[r_, :] + pv

    _for_blocks(i * ratio, lambda j: block(j, False), per_trip=4)
    for d in range(ratio):
        block(i * ratio + d, True)

    lp = lam_ref[...]
    s1 = jnp.sum(lp[0:1, :] * lp[1:2, :], axis=-1, keepdims=True)
    s2 = jnp.sum(lp[2:3, :] * lp[3:4, :], axis=-1, keepdims=True)
    lam = jnp.exp(s1) - jnp.exp(s2) + lam_init
    o_all = acc_scr[...] / l_scr[...]
    o = o_all[0:tq] - lam * o_all[tq:2 * tq]
    o_ref[...] = _rms(o) * g_ref[...] * (1.0 - lam_init)


def _diff_attention(proj, a_lambda, subln_g, bsz, t, lam_init, tq=512, tk=512):
    tq = min(tq, t)
    tk = min(tk, tq)
    nq = t // tq
    qo, ko, vo = OFF_AQ // LANES, OFF_AK // LANES, OFF_AV // LANES
    once = pl.Buffered(1)
    return pl.pallas_call(
        functools.partial(_diffattn_kernel, tq=tq, tk=tk, lam_init=lam_init),
        out_shape=jax.ShapeDtypeStruct((bsz * t, A_HEADS * LANES), F32),
        grid=(bsz, A_HEADS, nq),
        in_specs=[
            pl.BlockSpec((4, A_QK_DIM), lambda b, h, i: (0, 0)),
            pl.BlockSpec((1, LANES), lambda b, h, i: (0, 0)),
            pl.BlockSpec((tq, LANES), lambda b, h, i: (b * nq + i, qo + h)),
            pl.BlockSpec((t, LANES), lambda b, h, i: (b, ko + h), pipeline_mode=once),
            pl.BlockSpec((t, LANES), lambda b, h, i: (b, vo + h), pipeline_mode=once),
        ],
        out_specs=pl.BlockSpec((tq, LANES), lambda b, h, i: (b * nq + i, h)),
        scratch_shapes=[pltpu.VMEM((2 * tq, LANES), F32)] * 3,
        compiler_params=_cparams(("parallel", "parallel", "arbitrary")),
        name="diff_attention",
    )(a_lambda, subln_g.reshape(1, LANES), proj, proj, proj)


def _gdn_prep_kernel(qkv_ref, misc_ref, cw_ref, gp_ref,
                     u_ref, w_ref, qd_ref, kdt_ref, attn_ref, gl_ref, xs_scr):
    c_rows = GDN_CHUNK
    rows = c_rows * GDN_SUB
    n = pl.program_id(1)

    @pl.when(n == 0)
    def _():
        xs_scr[0:8, :] = jnp.zeros((8, xs_scr.shape[1]), F32)

    xs_scr[8:8 + rows, :] = qkv_ref[...]
    cw = cw_ref[...]
    y = cw[0:1, :] * xs_scr[5:5 + rows, :]
    for jj in range(1, CONV_K):
        y = y + cw[jj:jj + 1, :] * xs_scr[5 + jj:5 + jj + rows, :]
    xs_scr[0:8, :] = xs_scr[rows:rows + 8, :]
    y = y * _sigmoid(y)

    mi = misc_ref[...]
    gp = gp_ref[...]
    xg = mi + gp[1:2, :]
    softplus = jnp.maximum(xg, 0.0) + jnp.log(1.0 + jnp.exp(-jnp.abs(xg)))
    g_all = -jnp.exp(gp[0:1, :]) * softplus
    beta_all = _sigmoid(mi)

    r = lax.broadcasted_iota(I32, (c_rows, c_rows), 0)
    c = lax.broadcasted_iota(I32, (c_rows, c_rows), 1)
    incl = r >= c
    strict = r > c
    eye = (r == c).astype(F32)
    ones_l = incl.astype(BF16)
    g_hi = g_all.astype(BF16)
    g_r1 = g_all - g_hi.astype(F32)
    g_mid = g_r1.astype(BF16)
    g_lo = (g_r1 - g_mid.astype(F32)).astype(BF16)
    subs = [slice(sc * c_rows, (sc + 1) * c_rows) for sc in range(GDN_SUB)]
    gcs = [_dot(ones_l, g_hi[rs_]) + (_dot(ones_l, g_mid[rs_]) + _dot(ones_l, g_lo[rs_]))
           for rs_ in subs]
    gcts = [gc.T for gc in gcs]

    heads = range(GDN_SUB * B_HEADS)
    qs, ks, vs, gcols, bcols, decays, kbs, khbs = [], [], [], [], [], [], [], []
    for inst in heads:
        sc, h = divmod(inst, B_HEADS)
        rs_ = subs[sc]
        qh = y[rs_, h * B_DIM:(h + 1) * B_DIM]
        kh = y[rs_, 512 + h * B_DIM:512 + (h + 1) * B_DIM]
        qs.append(qh * lax.rsqrt(jnp.sum(qh * qh, axis=-1, keepdims=True) + NORM_EPS) * (B_DIM ** -0.5))
        kh = kh * lax.rsqrt(jnp.sum(kh * kh, axis=-1, keepdims=True) + NORM_EPS)
        ks.append(kh)
        vs.append(y[rs_, 1024 + h * B_DIM:1024 + (h + 1) * B_DIM])
        gcol = gcs[sc][:, MISC_A + h:MISC_A + h + 1]
        grow = gcts[sc][MISC_A + h:MISC_A + h + 1, :]
        gcols.append(gcol)
        bcols.append(beta_all[rs_, MISC_B + h:MISC_B + h + 1])
        decays.append(jnp.where(incl, jnp.exp(jnp.where(incl, gcol - grow, 0.0)), 0.0))
        kbs.append(kh * bcols[inst])
        khbs.append(kh.astype(BF16))
    lows = [jnp.where(strict, _dot(kbs[h].astype(BF16), khbs[h], _NT) * decays[h], 0.0) for h in heads]
    attns = [_dot(qs[h].astype(BF16), khbs[h], _NT) * decays[h] for h in heads]

    same16 = (r >> 4) == (c >> 4)
    pws = [jnp.where(same16, lows[h], 0.0) for h in heads]
    invs = [eye - pws[h] for h in heads]
    for _ in range(3):
        pwbs = [pws[h].astype(BF16) for h in heads]
        pws = [_dot(pwbs[h], pwbs[h]) for h in heads]
        invs = [invs[h] + _dot(invs[h].astype(BF16), pws[h].astype(BF16)) for h in heads]
    for sh in (4, 5, 6):
        pair = ((r >> (sh + 1)) == (c >> (sh + 1))) & ((r >> sh) != (c >> sh))
        invbs = [invs[h].astype(BF16) for h in heads]
        mids = [_dot(jnp.where(pair, lows[h], 0.0).astype(BF16), invbs[h]).astype(BF16) for h in heads]
        invs = [invs[h] - _dot(invbs[h], mids[h]) for h in heads]

    egs = [jnp.exp(gcols[h]) for h in heads]
    sols = [_dot(invs[h].astype(BF16),
                 jnp.concatenate([vs[h] * bcols[h], kbs[h] * egs[h]], axis=1).astype(BF16)) for h in heads]
    for inst in heads:
        sc, h = divmod(inst, B_HEADS)
        rs_ = subs[sc]
        sl = slice(h * B_DIM, (h + 1) * B_DIM)
        glast = gcs[sc][c_rows - 1:c_rows, MISC_A + h:MISC_A + h + 1]
        u_ref[rs_, sl] = sols[inst][:, :B_DIM]
        w_ref[rs_, sl] = sols[inst][:, B_DIM:]
        qd_ref[rs_, sl] = qs[inst] * egs[inst]
        kdt_ref[rs_, sl] = (ks[inst] * jnp.exp(glast - gcols[inst])).T
        attn_ref[rs_, sl] = attns[inst]
        gl_ref[sc * 8:(sc + 1) * 8, sl] = jnp.broadcast_to(jnp.exp(glast), (8, B_DIM))


def _gdn_scan_kernel(u_ref, w_ref, qd_ref, kdt_ref, attn_ref, gl_ref, z_ref, g_ref, o_ref, s_scr):
    n = pl.program_id(1)

    @pl.when(n == 0)
    def _():
        s_scr[...] = jnp.zeros(s_scr.shape, F32)

    g = g_ref[...]
    heads = range(B_HEADS)
    sls = [slice(h * B_DIM, (h + 1) * B_DIM) for h in heads]
    ss = [s_scr[h] for h in heads]
    for sc in range(GDN_SUB):
        rs_ = slice(sc * GDN_CHUNK, (sc + 1) * GDN_CHUNK)
        sbs = [ss[h].astype(BF16) for h in heads]
        wss = [_dot(w_ref[rs_, sls[h]].astype(BF16), sbs[h]) for h in heads]
        qss = [_dot(qd_ref[rs_, sls[h]].astype(BF16), sbs[h]) for h in heads]
        vbs = [(u_ref[rs_, sls[h]] - wss[h]).astype(BF16) for h in heads]
        os_ = [qss[h] + _dot(attn_ref[rs_, sls[h]].astype(BF16), vbs[h]) for h in heads]
        kvs = [_dot(kdt_ref[rs_, sls[h]].astype(BF16), vbs[h]) for h in heads]
        ss = [ss[h] * gl_ref[sc * 8:sc * 8 + 1, sls[h]] + kvs[h] for h in heads]
        for h in heads:
            z = z_ref[rs_, sls[h]]
            o_ref[rs_, sls[h]] = _rms(os_[h]) * g * (z * _sigmoid(z))
    for h in heads:
        s_scr[h] = ss[h]


def _gated_deltanet(proj, conv_w, a_log, dt_bias, norm_g, bsz, t):
    m = bsz * t
    c_rows = GDN_CHUNK
    n = t // c_rows
    width = B_HEADS * B_DIM
    gp = jnp.zeros((8, LANES), F32)
    gp = gp.at[0, MISC_A:MISC_A + B_HEADS].set(a_log.astype(F32))
    gp = gp.at[1, MISC_A:MISC_A + B_HEADS].set(dt_bias.astype(F32))
    sub = min(GDN_SUB, n)
    assert sub == GDN_SUB and n % sub == 0
    n2 = n // sub
    row2 = lambda b, i: (b * n2 + i, 0)
    big2 = pl.BlockSpec((sub * c_rows, width), row2)
    u, w, qd, kdt, attn, gl = pl.pallas_call(
        _gdn_prep_kernel,
        out_shape=[jax.ShapeDtypeStruct((m, width), F32)] * 5
        + [jax.ShapeDtypeStruct((bsz * n * 8, width), F32)],
        grid=(bsz, n2),
        in_specs=[pl.BlockSpec((sub * c_rows, 3 * width), lambda b, i: (b * n2 + i, OFF_BQKV // (3 * width))),
                  pl.BlockSpec((sub * c_rows, LANES), lambda b, i: (b * n2 + i, OFF_MISC // LANES)),
                  pl.BlockSpec((CONV_K, 3 * width), lambda b, i: (0, 0)),
                  pl.BlockSpec((8, LANES), lambda b, i: (0, 0))],
        out_specs=[big2] * 5 + [pl.BlockSpec((sub * 8, width), row2)],
        scratch_shapes=[pltpu.VMEM((sub * c_rows + 8, 3 * width), F32)],
        compiler_params=_cparams(("parallel", "arbitrary")),
        name="gdn_prep",
    )(proj, proj, conv_w, gp)
    return pl.pallas_call(
        _gdn_scan_kernel,
        out_shape=jax.ShapeDtypeStruct((m, width), F32),
        grid=(bsz, n2),
        in_specs=[big2] * 5 + [pl.BlockSpec((sub * 8, width), row2),
                               pl.BlockSpec((sub * c_rows, width), lambda b, i: (b * n2 + i, OFF_BZ // width)),
                               pl.BlockSpec((1, B_DIM), lambda b, i: (0, 0))],
        out_specs=big2,
        scratch_shapes=[pltpu.VMEM((B_HEADS, B_DIM, B_DIM), F32)],
        compiler_params=_cparams(("parallel", "arbitrary")),
        name="gdn_scan",
    )(u, w, qd, kdt, attn, gl, proj, norm_g.reshape(1, B_DIM))


def _dsa_prep_kernel(cq_ref, ckv_ref, misc_ref, qg_ref, kvg_ref, kig_ref, kib_ref,
                     wuq_ref, wuk_ref, wqi_ref, qa_ref, qi_ref, ckvn_ref, kpad_ref, wi_ref):
    tm = cq_ref.shape[0]
    ql = (_rms(cq_ref[...]) * qg_ref[...]).astype(BF16)
    qh = _dot(ql, wuq_ref[...]).astype(BF16)
    qa = (_dot(qh, wuk_ref[...]) * (C_HEAD_DIM ** -0.5 * LOG2E)).astype(BF16)
    qi = _dot(ql, wqi_ref[...]).astype(BF16)
    for h in range(C_HEADS):
        qa_ref[h * tm:(h + 1) * tm, :] = qa[:, h * LANES:(h + 1) * LANES]
        qi_ref[h * tm:(h + 1) * tm, :] = qi[:, h * LANES:(h + 1) * LANES]
    ckvn_ref[...] = (_rms(ckv_ref[...]) * kvg_ref[...]).astype(BF16)
    mi = misc_ref[...]
    lane = lax.broadcasted_iota(I32, mi.shape, 1)
    isk = lane < IDX_DIM
    mu = jnp.sum(jnp.where(isk, mi, 0.0), axis=-1, keepdims=True) * (1.0 / IDX_DIM)
    xc = jnp.where(isk, mi - mu, 0.0)
    var = jnp.sum(xc * xc, axis=-1, keepdims=True) * (1.0 / IDX_DIM)
    kn = xc * lax.rsqrt(var + NORM_EPS) * kig_ref[...] + kib_ref[...]
    kpad_ref[...] = jnp.where(isk, kn, 0.0).astype(BF16)
    wi_ref[...] = mi * (IDX_HEADS ** -0.5 * IDX_DIM ** -0.5)


def _float_key(x):
    b = lax.bitcast_convert_type(x, I32)
    return jnp.where(b >= 0, b, b ^ INT_MAX)


def _bit_transpose32(words):
    w = list(words)
    j, m = 16, 0x0000FFFF
    while j:
        k = 0
        while k < 32:
            t = (w[k] ^ lax.shift_right_logical(w[k + j], j)) & m
            w[k] = w[k] ^ t
            w[k + j] = w[k + j] ^ lax.shift_left(t, j)
            k = (k + j + 1) & ~j
        j >>= 1
        m = (m ^ (m << j)) & 0x7FFFFFFF
    return w


def _dsa_main_kernel(qa_ref, qi_ref, wi_ref, kpad_ref, ckv_ref, wuv_ref, o_ref,
                     key_scr, alive_scr, gt_scr, sel_scr, m_scr, l_scr, acc_scr,
                     *, tq, tk, n_sel, idx_bits):
    i = pl.program_id(1)
    nvis = ((i + 1) * tq + tk - 1) // tk
    tiles = tk // LANES
    group = 32 // tiles
    span = 32 * LANES
    ngrp = (nvis + group - 1) // group
    rows = i * tq + lax.broadcasted_iota(I32, (tq, tk), 0)
    col0 = lax.broadcasted_iota(I32, (tq, tk), 1)
    lane = lax.broadcasted_iota(I32, (tq, LANES), 1)
    wi = wi_ref[...]

    def score_block(jb, carry):
        start = pl.multiple_of(jb * tk, tk)
        kblk = kpad_ref[pl.ds(start, tk), :]
        hc = IDX_HEADS // ATT_CHAINS
        ss = [_dot(qi_ref[ch * hc * tq:(ch + 1) * hc * tq, :], kblk, _NT) for ch in range(ATT_CHAINS)]
        acc = None
        for h in range(IDX_HEADS):
            s_h = ss[h // hc][(h % hc) * tq:(h % hc + 1) * tq]
            term = jnp.maximum(s_h, 0.0) * wi[:, MISC_WI + h:MISC_WI + h + 1]
            acc = term if acc is None else acc + term
        key = jnp.where(col0 + start <= rows, _float_key(acc) ^ INT_MIN, 0)
        for tt in range(tiles):
            key_scr[jb * tiles + tt] = key[:, tt * LANES:(tt + 1) * LANES]
        return carry

    _for_blocks(nvis, lambda jb: score_block(jb, 0), per_trip=4)

    def fill_tile(kt, carry):
        key_scr[kt] = jnp.zeros((tq, LANES), I32)
        return carry

    lax.fori_loop(nvis * tiles, ngrp * 32, fill_tile, 0)

    def plane_group(g, carry):
        def plane_rows(rg, carry2):
            r0 = pl.multiple_of(rg * 8, 8)
            planes = _bit_transpose32([key_scr[g * 32 + k, pl.ds(r0, 8), :] for k in range(32)])
            for b in range(32):
                key_scr[g * 32 + b, pl.ds(r0, 8), :] = planes[b]
            return carry2
        lax.fori_loop(0, tq // 8, plane_rows, 0)
        alive_scr[g] = jnp.full((tq, LANES), -1, I32)
        gt_scr[g] = jnp.zeros((tq, LANES), I32)
        return carry

    lax.fori_loop(0, ngrp, plane_group, 0)

    def popcount_rows(mask_fn):
        def body(g, acc):
            return acc + lax.population_count(mask_fn(g))
        acc = lax.fori_loop(0, ngrp, body, jnp.zeros((tq, LANES), I32))
        return jnp.sum(acc.astype(F32), axis=-1, keepdims=True)

    chains = [slice(ch * tq // RADIX_CHAINS, (ch + 1) * tq // RADIX_CHAINS) for ch in range(RADIX_CHAINS)]

    def run_radix(ng):
        def radix_step(it, carry):
            needs, ones = carry
            takes = [o >= n_ for o, n_ in zip(ones, needs)]
            flips = [jnp.where(t_, 0, -1) for t_ in takes]
            nxt = jnp.minimum(it + 1, 31)
            accs = [jnp.zeros((tq // RADIX_CHAINS, LANES), I32) for _ in chains]
            for g in range(ng):
                for ch, r_ in enumerate(chains):
                    a = alive_scr[g, r_, :]
                    p = key_scr[g * 32 + it, r_, :]
                    gt_scr[g, r_, :] = gt_scr[g, r_, :] | (a & p & flips[ch])
                    a = a & (p ^ flips[ch])
                    alive_scr[g, r_, :] = a
                    accs[ch] = accs[ch] + lax.population_count(a & key_scr[g * 32 + nxt, r_, :])
            new_ones = tuple(jnp.sum(acc.astype(F32), axis=-1, keepdims=True) for acc in accs)
            new_needs = tuple(jnp.where(t_, n_, n_ - o) for t_, n_, o in zip(takes, needs, ones))
            return new_needs, new_ones

        def go():
            ones0 = []
            for r_ in chains:
                acc = lax.population_count(key_scr[0, r_, :])
                for g in range(1, ng):
                    acc = acc + lax.population_count(key_scr[g * 32, r_, :])
                ones0.append(jnp.sum(acc.astype(F32), axis=-1, keepdims=True))
            need0 = tuple(jnp.full((tq // RADIX_CHAINS, 1), float(n_sel), F32) for _ in chains)
            needs, _ = lax.fori_loop(0, 32, radix_step, (need0, tuple(ones0)))
            return jnp.concatenate(needs, axis=0)
        return go

    max_grp = key_scr.shape[0] // 32
    need = lax.switch(ngrp - 1, [run_radix(ng) for ng in range(1, max_grp + 1)])
    ties = popcount_rows(lambda g: alive_scr[g])

    def cols_below(g, x):
        nk = jnp.clip((x - g * span - lane + (LANES - 1)) >> 7, 0, 32)
        top = ~lax.shift_right_logical(jnp.full((tq, LANES), -1, I32), jnp.minimum(nk, 31))
        return jnp.where(nk >= 32, -1, top)

    qpos = i * tq + lax.broadcasted_iota(I32, (tq, 1), 0)
    few = qpos < n_sel
    overflow = jnp.where(few, 0.0, ties - need)
    xcut_all = jnp.full((tq, 1), 1 << 30, I32)

    def resolve_ties():
        def cut_step(it, cut):
            cand = cut | lax.shift_left(jnp.int32(1), idx_bits - 1 - it)
            below = popcount_rows(lambda g: alive_scr[g] & cols_below(g, cand))
            return jnp.where(below < need, cand, cut)

        cut = lax.fori_loop(0, idx_bits, cut_step, jnp.zeros((tq, 1), I32))
        return jnp.where(overflow > 0.0, cut + 1, xcut_all)

    xcut = lax.cond(jnp.max(overflow) > 0.0, resolve_ties, lambda: xcut_all)

    def select_group(g, carry):
        chosen = gt_scr[g] | (alive_scr[g] & cols_below(g, xcut))
        sel_scr[g] = jnp.where(few, cols_below(g, qpos + 1), chosen)
        return carry

    lax.fori_loop(0, ngrp, select_group, 0)

    m_scr[...] = jnp.full(m_scr.shape, NEG, F32)
    l_scr[...] = jnp.zeros(l_scr.shape, F32)
    acc_scr[...] = jnp.zeros(acc_scr.shape, F32)

    def attend_block(jb, carry):
        start = pl.multiple_of(jb * tk, tk)
        sel = sel_scr[jb // group]
        k0 = (jb % group) * tiles
        bias = jnp.concatenate(
            [jnp.where(jnp.left_shift(sel, k0 + tt) < 0, 0.0, NEG) for tt in range(tiles)], axis=1)
        ckv = ckv_ref[pl.ds(start, tk), :]
        hc = C_HEADS // ATT_CHAINS
        rs = [slice(ch * hc * tq, (ch + 1) * hc * tq) for ch in range(ATT_CHAINS)]
        ss = [_dot(qa_ref[r_, :], ckv, _NT) for r_ in rs]
        ss = [(s.reshape(hc, tq, tk) + bias[None]).reshape(hc * tq, tk) for s in ss]
        m_prevs = [m_scr[r_, :] for r_ in rs]
        m_news = [jnp.maximum(mp, jnp.max(s, axis=-1, keepdims=True)) for mp, s in zip(m_prevs, ss)]
        alphas = [jnp.exp2(mp - mn) for mp, mn in zip(m_prevs, m_news)]
        ps = [jnp.exp2(s - jnp.tile(mn, (1, tiles))) for s, mn in zip(ss, m_news)]
        pvs = [_dot(p.astype(BF16), ckv) for p in ps]
        for r_, mn, al, p, pv in zip(rs, m_news, alphas, ps, pvs):
            l_scr[r_, :] = al * l_scr[r_, :] + jnp.sum(p, axis=-1, keepdims=True)
            m_scr[r_, :] = mn
            acc_scr[r_, :] = al * acc_scr[r_, :] + pv
        return carry

    _for_blocks(nvis, lambda jb: attend_block(jb, 0), per_trip=4)
    o_all = acc_scr[...] / l_scr[...]
    o_lat = jnp.concatenate([o_all[h * tq:(h + 1) * tq] for h in range(C_HEADS)], axis=1)
    o_ref[...] = _dot(o_lat.astype(BF16), wuv_ref[...])


def _dsa_attention(proj, q_norm_g, kv_norm_g, kidx_g, kidx_b, w_uq, w_qidx, w_uk, w_uv,
                   bsz, t, tq=256, tk=512):
    m = bsz * t
    tq = min(tq, t)
    tk = min(tk, t)
    tm = tq
    hr = C_HEADS * C_KV_LORA
    wuq = w_uq.reshape(C_Q_LORA, C_HEADS * C_HEAD_DIM).astype(BF16)
    eye_h = jnp.eye(C_HEADS, dtype=F32)
    wuk_bd = jnp.einsum('rhd,hg->hdgr', w_uk, eye_h).reshape(C_HEADS * C_HEAD_DIM, hr).astype(BF16)
    wuv_bd = jnp.einsum('rhd,hg->hrgd', w_uv, eye_h).reshape(hr, C_HEADS * C_HEAD_DIM).astype(BF16)
    wqi = jnp.pad(w_qidx, ((0, 0), (0, 0), (0, LANES - IDX_DIM))).reshape(C_Q_LORA, IDX_HEADS * LANES).astype(BF16)
    pad_row = lambda v: jnp.pad(v.astype(F32), (0, LANES - IDX_DIM)).reshape(1, LANES)
    full = lambda shape: pl.BlockSpec(shape, lambda i: (0, 0))
    qa, qi, ckvn, kpad, wi = pl.pallas_call(
        _dsa_prep_kernel,
        out_shape=[jax.ShapeDtypeStruct((m * C_HEADS, LANES), BF16), jax.ShapeDtypeStruct((m * IDX_HEADS, LANES), BF16),
                   jax.ShapeDtypeStruct((m, LANES), BF16), jax.ShapeDtypeStruct((m, LANES), BF16),
                   jax.ShapeDtypeStruct((m, LANES), F32)],
        grid=(m // tm,),
        in_specs=[pl.BlockSpec((tm, C_Q_LORA), lambda i: (i, OFF_CQ // C_Q_LORA)),
                  pl.BlockSpec((tm, LANES), lambda i: (i, OFF_CKV // LANES)),
                  pl.BlockSpec((tm, LANES), lambda i: (i, OFF_MISC // LANES)),
                  full((1, C_Q_LORA)), full((1, LANES)), full((1, LANES)), full((1, LANES)),
                  full(wuq.shape), full(wuk_bd.shape), full(wqi.shape)],
        out_specs=[pl.BlockSpec((C_HEADS * tm, LANES), lambda i: (i, 0)),
                   pl.BlockSpec((IDX_HEADS * tm, LANES), lambda i: (i, 0)),
                   pl.BlockSpec((tm, LANES), lambda i: (i, 0)), pl.BlockSpec((tm, LANES), lambda i: (i, 0)),
                   pl.BlockSpec((tm, LANES), lambda i: (i, 0))],
        compiler_params=_cparams(("parallel",)),
        name="dsa_prep",
    )(proj, proj, proj, q_norm_g.reshape(1, C_Q_LORA), kv_norm_g.reshape(1, LANES),
      pad_row(kidx_g), pad_row(kidx_b), wuq, wuk_bd, wqi)

    nq = t // tq
    n_sel = min(INDEX_TOPK, t // 4)
    idx_bits = max(1, int(math.ceil(math.log2(t))))
    ngrp = -(-t // (32 * LANES))
    blk = lambda rows, w: pl.BlockSpec((rows, w), lambda b, i: (b * nq + i, 0))
    once = pl.Buffered(1)
    seq = lambda w: pl.BlockSpec((t, w), lambda b, i: (b, 0), pipeline_mode=once)
    return pl.pallas_call(
        functools.partial(_dsa_main_kernel, tq=tq, tk=tk, n_sel=n_sel, idx_bits=idx_bits),
        out_shape=jax.ShapeDtypeStruct((m, C_HEADS * C_HEAD_DIM), F32),
        grid=(bsz, nq),
        in_specs=[blk(C_HEADS * tq, LANES), blk(IDX_HEADS * tq, LANES), blk(tq, LANES), seq(LANES), seq(LANES),
                  pl.BlockSpec(wuv_bd.shape, lambda b, i: (0, 0), pipeline_mode=once)],
        out_specs=blk(tq, C_HEADS * C_HEAD_DIM),
        scratch_shapes=[pltpu.VMEM((ngrp * 32, tq, LANES), I32)]
        + [pltpu.VMEM((ngrp, tq, LANES), I32)] * 3
        + [pltpu.VMEM((C_HEADS * tq, LANES), F32)] * 3,
        compiler_params=_cparams(("parallel", "arbitrary")),
        name="dsa_main",
    )(qa, qi, wi, kpad, ckvn, wuv_bd)


def _merge_kernel(x_ref, g0_ref, g1_ref, g2_ref, ya_ref, yb_ref, yc_ref, bg_ref,
                  wa_ref, wb_ref, wc_ref, wo_ref, lg_ref, lb_ref, o_ref, *, alpha):
    bg = bg_ref[...]
    merged = None
    for idx, (g_ref, y_ref, w_ref) in enumerate(
            ((g0_ref, ya_ref, wa_ref), (g1_ref, yb_ref, wb_ref), (g2_ref, yc_ref, wc_ref))):
        gate = _sigmoid(g_ref[...] + bg[idx:idx + 1, :])
        term = gate * _dot(y_ref[...].astype(BF16), w_ref[...])
        merged = term if merged is None else merged + term
    y = alpha * x_ref[...] + _dot(merged.astype(BF16), wo_ref[...])
    o_ref[...] = _layer_norm(y, lg_ref[...], lb_ref[...])


def _merge(x, proj, ya, yb, yc, b_gate, wa, wb, wc, wo, ln_g, ln_b, alpha, tm=512):
    m, d = x.shape
    tm = min(tm, m)
    gate_blk = lambda k: pl.BlockSpec((tm, d), lambda i: (i, OFF_GATES // d + k))
    rowblk = lambda w: pl.BlockSpec((tm, w), lambda i: (i, 0))
    full = lambda a: pl.BlockSpec(a.shape, lambda i: (0, 0))
    bg = b_gate.reshape(3, d)
    ws = [wa.astype(BF16), wb.astype(BF16), wc.astype(BF16), wo.astype(BF16)]
    lg, lb = ln_g.reshape(1, d), ln_b.reshape(1, d)
    return pl.pallas_call(
        functools.partial(_merge_kernel, alpha=alpha),
        out_shape=jax.ShapeDtypeStruct((m, d), F32),
        grid=(m // tm,),
        in_specs=[rowblk(d), gate_blk(0), gate_blk(1), gate_blk(2),
                  rowblk(ya.shape[1]), rowblk(yb.shape[1]), rowblk(yc.shape[1]), full(bg)]
        + [full(w) for w in ws] + [full(lg), full(lb)],
        out_specs=rowblk(d),
        compiler_params=_cparams(("parallel",)),
        name="merge",
    )(x, proj, proj, proj, ya, yb, yc, bg, *ws, lg, lb)


def _ffn_kernel(x_ref, w1_ref, w2_ref, lg_ref, lb_ref, o_ref, acc_scr, *, alpha):
    k = pl.program_id(1)

    @pl.when(k == 0)
    def _():
        acc_scr[...] = jnp.zeros(acc_scr.shape, F32)

    h = jnp.maximum(_dot(x_ref[...].astype(BF16), w1_ref[...]), 0.0)
    acc_scr[...] += _dot((h * h).astype(BF16), w2_ref[...])

    @pl.when(k == pl.num_programs(1) - 1)
    def _():
        o_ref[...] = _layer_norm(alpha * x_ref[...] + acc_scr[...], lg_ref[...], lb_ref[...])


def _ffn(x, w1, w2, ln_g, ln_b, alpha, tm=1024, tf=1024):
    m, d = x.shape
    tm = min(tm, m)
    dff = w1.shape[1]
    return pl.pallas_call(
        functools.partial(_ffn_kernel, alpha=alpha),
        out_shape=jax.ShapeDtypeStruct((m, d), F32),
        grid=(m // tm, dff // tf),
        in_specs=[pl.BlockSpec((tm, d), lambda i, k: (i, 0)),
                  pl.BlockSpec((d, tf), lambda i, k: (0, k)),
                  pl.BlockSpec((tf, d), lambda i, k: (k, 0)),
                  pl.BlockSpec((1, d), lambda i, k: (0, 0)),
                  pl.BlockSpec((1, d), lambda i, k: (0, 0))],
        out_specs=pl.BlockSpec((tm, d), lambda i, k: (i, 0)),
        scratch_shapes=[pltpu.VMEM((tm, d), F32)],
        compiler_params=_cparams(("parallel", "arbitrary")),
        name="ffn",
    )(x, w1.astype(BF16), w2.astype(BF16), ln_g.reshape(1, d), ln_b.reshape(1, d))


def kernel(x, w_in, b_gate, a_lambda, a_subln_g, b_conv_w, b_a_log, b_dt_bias, b_norm_g,
           c_q_norm_g, c_kv_norm_g, c_kidx_g, c_kidx_b, c_w_uq, c_w_qidx, c_w_uk, c_w_uv,
           w_branch_a, w_branch_b, w_branch_c, w_o, ln1_g, ln1_b, w_ff1, w_ff2, ln2_g, ln2_b):
    bsz, t, d = x.shape
    depth = w_in.shape[0]
    alpha = (2 * depth) ** 0.25
    xf = x.reshape(bsz * t, d)
    for l in range(depth):
        lam_init = 0.8 - 0.6 * math.exp(-0.3 * l)
        proj = _matmul(xf, _pack_w_in(w_in[l]), F32, tm=min(1024, bsz * t), tn=1024)
        y_a = _diff_attention(proj, a_lambda[l], a_subln_g[l], bsz, t, lam_init)
        y_b = _gated_deltanet(proj, b_conv_w[l], b_a_log[l], b_dt_bias[l], b_norm_g[l], bsz, t)
        y_c = _dsa_attention(proj, c_q_norm_g[l], c_kv_norm_g[l], c_kidx_g[l], c_kidx_b[l],
                             c_w_uq[l], c_w_qidx[l], c_w_uk[l], c_w_uv[l], bsz, t)
        xf = _merge(xf, proj, y_a, y_b, y_c, b_gate[l], w_branch_a[l], w_branch_b[l], w_branch_c[l],
                    w_o[l], ln1_g[l], ln1_b[l], alpha)
        xf = _ffn(xf, w_ff1[l], w_ff2[l], ln2_g[l], ln2_b[l], alpha)
    return xf.reshape(bsz, t, d)
```

```python
import functools
import math

import jax
import jax.numpy as jnp
from jax import lax
from jax.experimental import pallas as pl
from jax.experimental.pallas import tpu as pltpu

F32 = jnp.float32
BF16 = jnp.bfloat16
I32 = jnp.int32

LANES = 128
NEG = -1e30
LOG2E = 1.4426950408889634
INT_MIN = -2 ** 31
INT_MAX = 2 ** 31 - 1
VMEM_LIMIT = 56 * 1024 * 1024

A_HEADS = 4
A_QK_DIM = 64
B_HEADS = 4
B_DIM = 128
CONV_K = 4
GDN_CHUNK = 128
GDN_SUB = 4
C_HEADS = 8
C_HEAD_DIM = 64
C_Q_LORA = 256
C_KV_LORA = 128
IDX_HEADS = 8
IDX_DIM = 64
INDEX_TOPK = 256
NORM_EPS = 1e-6
RADIX_CHAINS = 2
ATT_CHAINS = 2

OFF_BQKV, OFF_AQ, OFF_BZ = 0, 1536, 2048
OFF_CQ, OFF_CKV, OFF_MISC, OFF_GATES = 2560, 2816, 2944, 3072
OFF_AK, OFF_AV = 0, 512
MISC_WI, MISC_A, MISC_B = 64, 72, 76

_NT = (((1,), (1,)), ((), ()))
_NN = (((1,), (0,)), ((), ()))


def _cparams(sem):
    return pltpu.CompilerParams(dimension_semantics=sem, vmem_limit_bytes=VMEM_LIMIT)


def _dot(a, b, dims=_NN):
    return lax.dot_general(a, b, dims, preferred_element_type=F32)


def _for_blocks(n, body, per_trip=2):
    def trip(jj, carry):
        for u in range(per_trip):
            body(per_trip * jj + u)
        return carry

    lax.fori_loop(0, n // per_trip, trip, 0)
    done = (n // per_trip) * per_trip
    step = per_trip // 2
    while step:
        has = (n - done) >= step

        @pl.when(has)
        def _(done=done, step=step):
            for u in range(step):
                body(done + u)
        done = done + jnp.where(has, step, 0)
        step //= 2


def _layer_norm(x, g, b):
    mu = jnp.mean(x, axis=-1, keepdims=True)
    xc = x - mu
    var = jnp.mean(xc * xc, axis=-1, keepdims=True)
    return xc * lax.rsqrt(var + NORM_EPS) * g + b


def _rms(x):
    return x * lax.rsqrt(jnp.mean(x * x, axis=-1, keepdims=True) + NORM_EPS)


def _sigmoid(x):
    return 1.0 / (1.0 + jnp.exp(-x))


def _mm_kernel(x_ref, w_ref, o_ref):
    o_ref[...] = _dot(x_ref[...].astype(BF16), w_ref[...]).astype(o_ref.dtype)


def _matmul(x, w, out_dtype, tm, tn):
    m, k = x.shape
    n = w.shape[1]
    return pl.pallas_call(
        _mm_kernel,
        out_shape=jax.ShapeDtypeStruct((m, n), out_dtype),
        grid=(m // tm, n // tn),
        in_specs=[pl.BlockSpec((tm, k), lambda i, j: (i, 0)),
                  pl.BlockSpec((k, tn), lambda i, j: (0, j))],
        out_specs=pl.BlockSpec((tm, tn), lambda i, j: (i, j)),
        compiler_params=_cparams(("parallel", "parallel")),
        name="in_proj",
    )(x, w)


def _pack_w_in(w):
    d = w.shape[0]
    misc = jnp.concatenate(
        [w[:, 3976:4040], w[:, 4040:4048], w[:, 3584:3588], w[:, 3588:3592],
         jnp.zeros((d, LANES - 80), w.dtype)], axis=1)
    packed = jnp.concatenate(
        [w[:, 1536:3072], w[:, 0:512], w[:, 3072:3584], w[:, 3592:3848], w[:, 3848:3976], misc,
         w[:, 4048:7120]], axis=1)
    return packed.astype(BF16), w[:, 512:1536].astype(BF16)


def _diffattn_kernel(lam_ref, g_ref, q_ref, k_ref, v_ref, o_ref, m_scr, l_scr, acc_scr, *, tq, tk, lam_init):
    i = pl.program_id(2)
    tiles = tk // LANES
    ratio = tq // tk
    q = q_ref[...] * (A_QK_DIM ** -0.5 * LOG2E)
    lane = lax.broadcasted_iota(I32, q.shape, 1)
    q2 = jnp.concatenate([jnp.where(lane < A_QK_DIM, q, 0.0),
                          jnp.where(lane >= A_QK_DIM, q, 0.0)], axis=0).astype(BF16)
    m_scr[...] = jnp.full(m_scr.shape, NEG, F32)
    l_scr[...] = jnp.zeros(l_scr.shape, F32)
    acc_scr[...] = jnp.zeros(acc_scr.shape, F32)

    def block(j, diagonal):
        start = pl.multiple_of(j * tk, tk)
        kb = k_ref[pl.ds(start, tk), :]
        vb = v_ref[pl.ds(start, tk), :]
        rs = [slice(0, tq), slice(tq, 2 * tq)]
        ss = [_dot(q2[r_], kb, _NT) for r_ in rs]
        if diagonal:
            r = lax.broadcasted_iota(I32, (tq, tk), 0)
            cc = lax.broadcasted_iota(I32, (tq, tk), 1)
            bias = jnp.where(cc + (j - i * ratio) * tk <= r, 0.0, NEG)
            ss = [s + bias for s in ss]
        m_prevs = [m_scr[r_, :] for r_ in rs]
        m_news = [jnp.maximum(mp, jnp.max(s, axis=-1, keepdims=True)) for mp, s in zip(m_prevs, ss)]
        alphas = [jnp.exp2(mp - mn) for mp, mn in zip(m_prevs, m_news)]
        ps = [jnp.exp2(s - jnp.tile(mn, (1, tiles))) for s, mn in zip(ss, m_news)]
        pvs = [_dot(p.astype(BF16), vb) for p in ps]
        for r_, mn, al, p, pv in zip(rs, m_news, alphas, ps, pvs):
            l_scr[r_, :] = al * l_scr[r_, :] + jnp.sum(p, axis=-1, keepdims=True)
            m_scr[r_, :] = mn
            acc_scr[r_, :] = al * acc_scr[r_, :] + pv

    _for_blocks(i * ratio, lambda j: block(j, False), per_trip=4)
    for d in range(ratio):
        block(i * ratio + d, True)

    lp = lam_ref[...]
    s1 = jnp.sum(lp[0:1, :] * lp[1:2, :], axis=-1, keepdims=True)
    s2 = jnp.sum(lp[2:3, :] * lp[3:4, :], axis=-1, keepdims=True)
    lam = jnp.exp(s1) - jnp.exp(s2) + lam_init
    o_all = acc_scr[...] / l_scr[...]
    o = o_all[0:tq] - lam * o_all[tq:2 * tq]
    o_ref[...] = _rms(o) * g_ref[...] * (1.0 - lam_init)


def _diff_attention(proj, proj_kv, a_lambda, subln_g, bsz, t, lam_init, tq=512, tk=512):
    tq = min(tq, t)
    tk = min(tk, tq)
    nq = t // tq
    qo, ko, vo = OFF_AQ // LANES, OFF_AK // LANES, OFF_AV // LANES
    return pl.pallas_call(
        functools.partial(_diffattn_kernel, tq=tq, tk=tk, lam_init=lam_init),
        out_shape=jax.ShapeDtypeStruct((bsz * t, A_HEADS * LANES), F32),
        grid=(bsz, A_HEADS, nq),
        in_specs=[
            pl.BlockSpec((4, A_QK_DIM), lambda b, h, i: (0, 0)),
            pl.BlockSpec((1, LANES), lambda b, h, i: (0, 0)),
            pl.BlockSpec((tq, LANES), lambda b, h, i: (b * nq + i, qo + h)),
            pl.BlockSpec((t, LANES), lambda b, h, i: (b, ko + h)),
            pl.BlockSpec((t, LANES), lambda b, h, i: (b, vo + h)),
        ],
        out_specs=pl.BlockSpec((tq, LANES), lambda b, h, i: (b * nq + i, h)),
        scratch_shapes=[pltpu.VMEM((2 * tq, LANES), F32)] * 3,
        compiler_params=_cparams(("parallel", "parallel", "arbitrary")),
        name="diff_attention",
    )(a_lambda, subln_g.reshape(1, LANES), proj, proj_kv, proj_kv)


def _gdn_prep_kernel(qkv_ref, misc_ref, cw_ref, gp_ref,
                     u_ref, w_ref, qd_ref, kdt_ref, attn_ref, gl_ref, xs_scr):
    c_rows = GDN_CHUNK
    rows = c_rows * GDN_SUB
    n = pl.program_id(1)

    @pl.when(n == 0)
    def _():
        xs_scr[0:8, :] = jnp.zeros((8, xs_scr.shape[1]), F32)

    xs_scr[8:8 + rows, :] = qkv_ref[...]
    cw = cw_ref[...]
    y = cw[0:1, :] * xs_scr[5:5 + rows, :]
    for jj in range(1, CONV_K):
        y = y + cw[jj:jj + 1, :] * xs_scr[5 + jj:5 + jj + rows, :]
    xs_scr[0:8, :] = xs_scr[rows:rows + 8, :]
    y = y * _sigmoid(y)

    mi = misc_ref[...]
    gp = gp_ref[...]
    xg = mi + gp[1:2, :]
    softplus = jnp.maximum(xg, 0.0) + jnp.log(1.0 + jnp.exp(-jnp.abs(xg)))
    g_all = -jnp.exp(gp[0:1, :]) * softplus
    beta_all = _sigmoid(mi)

    r = lax.broadcasted_iota(I32, (c_rows, c_rows), 0)
    c = lax.broadcasted_iota(I32, (c_rows, c_rows), 1)
    incl = r >= c
    strict = r > c
    eye = (r == c).astype(F32)
    ones_l = incl.astype(BF16)
    g_hi = g_all.astype(BF16)
    g_r1 = g_all - g_hi.astype(F32)
    g_mid = g_r1.astype(BF16)
    g_lo = (g_r1 - g_mid.astype(F32)).astype(BF16)
    subs = [slice(sc * c_rows, (sc + 1) * c_rows) for sc in range(GDN_SUB)]
    gcs = [_dot(ones_l, g_hi[rs_]) + (_dot(ones_l, g_mid[rs_]) + _dot(ones_l, g_lo[rs_]))
           for rs_ in subs]
    gcts = [gc.T for gc in gcs]

    heads = range(GDN_SUB * B_HEADS)
    qs, ks, vs, gcols, bcols, decays, kbs, khbs = [], [], [], [], [], [], [], []
    for inst in heads:
        sc, h = divmod(inst, B_HEADS)
        rs_ = subs[sc]
        qh = y[rs_, h * B_DIM:(h + 1) * B_DIM]
        kh = y[rs_, 512 + h * B_DIM:512 + (h + 1) * B_DIM]
        qs.append(qh * lax.rsqrt(jnp.sum(qh * qh, axis=-1, keepdims=True) + NORM_EPS) * (B_DIM ** -0.5))
        kh = kh * lax.rsqrt(jnp.sum(kh * kh, axis=-1, keepdims=True) + NORM_EPS)
        ks.append(kh)
        vs.append(y[rs_, 1024 + h * B_DIM:1024 + (h + 1) * B_DIM])
        gcol = gcs[sc][:, MISC_A + h:MISC_A + h + 1]
        grow = gcts[sc][MISC_A + h:MISC_A + h + 1, :]
        gcols.append(gcol)
        bcols.append(beta_all[rs_, MISC_B + h:MISC_B + h + 1])
        decays.append(jnp.where(incl, jnp.exp(jnp.where(incl, gcol - grow, 0.0)), 0.0))
        kbs.append(kh * bcols[inst])
        khbs.append(kh.astype(BF16))
    lows = [jnp.where(strict, _dot(kbs[h].astype(BF16), khbs[h], _NT) * decays[h], 0.0) for h in heads]
    attns = [_dot(qs[h].astype(BF16), khbs[h], _NT) * decays[h] for h in heads]

    same16 = (r >> 4) == (c >> 4)
    pws = [jnp.where(same16, lows[h], 0.0) for h in heads]
    invs = [eye - pws[h] for h in heads]
    for _ in range(3):
        pwbs = [pws[h].astype(BF16) for h in heads]
        pws = [_dot(pwbs[h], pwbs[h]) for h in heads]
        invs = [invs[h] + _dot(invs[h].astype(BF16), pws[h].astype(BF16)) for h in heads]
    for sh in (4, 5, 6):
        pair = ((r >> (sh + 1)) == (c >> (sh + 1))) & ((r >> sh) != (c >> sh))
        invbs = [invs[h].astype(BF16) for h in heads]
        mids = [_dot(jnp.where(pair, lows[h], 0.0).astype(BF16), invbs[h]).astype(BF16) for h in heads]
        invs = [invs[h] - _dot(invbs[h], mids[h]) for h in heads]

    egs = [jnp.exp(gcols[h]) for h in heads]
    sols = [_dot(invs[h].astype(BF16),
                 jnp.concatenate([vs[h] * bcols[h], kbs[h] * egs[h]], axis=1).astype(BF16)) for h in heads]
    for inst in heads:
        sc, h = divmod(inst, B_HEADS)
        rs_ = subs[sc]
        sl = slice(h * B_DIM, (h + 1) * B_DIM)
        glast = gcs[sc][c_rows - 1:c_rows, MISC_A + h:MISC_A + h + 1]
        u_ref[rs_, sl] = sols[inst][:, :B_DIM]
        w_ref[rs_, sl] = sols[inst][:, B_DIM:]
        qd_ref[rs_, sl] = qs[inst] * egs[inst]
        kdt_ref[rs_, sl] = (ks[inst] * jnp.exp(glast - gcols[inst])).T
        attn_ref[rs_, sl] = attns[inst]
        gl_ref[sc * 8:(sc + 1) * 8, sl] = jnp.broadcast_to(jnp.exp(glast), (8, B_DIM))


def _gdn_scan_kernel(u_ref, w_ref, qd_ref, kdt_ref, attn_ref, gl_ref, z_ref, g_ref, o_ref, s_scr):
    n = pl.program_id(1)

    @pl.when(n == 0)
    def _():
        s_scr[...] = jnp.zeros(s_scr.shape, F32)

    g = g_ref[...]
    heads = range(B_HEADS)
    sls = [slice(h * B_DIM, (h + 1) * B_DIM) for h in heads]
    ss = [s_scr[h] for h in heads]
    for sc in range(GDN_SUB):
        rs_ = slice(sc * GDN_CHUNK, (sc + 1) * GDN_CHUNK)
        sbs = [ss[h].astype(BF16) for h in heads]
        wss = [_dot(w_ref[rs_, sls[h]].astype(BF16), sbs[h]) for h in heads]
        qss = [_dot(qd_ref[rs_, sls[h]].astype(BF16), sbs[h]) for h in heads]
        vbs = [(u_ref[rs_, sls[h]] - wss[h]).astype(BF16) for h in heads]
        os_ = [qss[h] + _dot(attn_ref[rs_, sls[h]].astype(BF16), vbs[h]) for h in heads]
        kvs = [_dot(kdt_ref[rs_, sls[h]].astype(BF16), vbs[h]) for h in heads]
        ss = [ss[h] * gl_ref[sc * 8:sc * 8 + 1, sls[h]] + kvs[h] for h in heads]
        for h in heads:
            z = z_ref[rs_, sls[h]]
            o_ref[rs_, sls[h]] = _rms(os_[h]) * g * (z * _sigmoid(z))
    for h in heads:
        s_scr[h] = ss[h]


def _gated_deltanet(proj, conv_w, a_log, dt_bias, norm_g, bsz, t):
    m = bsz * t
    c_rows = GDN_CHUNK
    n = t // c_rows
    width = B_HEADS * B_DIM
    gp = jnp.zeros((8, LANES), F32)
    gp = gp.at[0, MISC_A:MISC_A + B_HEADS].set(a_log.astype(F32))
    gp = gp.at[1, MISC_A:MISC_A + B_HEADS].set(dt_bias.astype(F32))
    sub = min(GDN_SUB, n)
    assert sub == GDN_SUB and n % sub == 0
    n2 = n // sub
    row2 = lambda b, i: (b * n2 + i, 0)
    big2 = pl.BlockSpec((sub * c_rows, width), row2)
    u, w, qd, kdt, attn, gl = pl.pallas_call(
        _gdn_prep_kernel,
        out_shape=[jax.ShapeDtypeStruct((m, width), F32)] * 5
        + [jax.ShapeDtypeStruct((bsz * n * 8, width), F32)],
        grid=(bsz, n2),
        in_specs=[pl.BlockSpec((sub * c_rows, 3 * width), lambda b, i: (b * n2 + i, OFF_BQKV // (3 * width))),
                  pl.BlockSpec((sub * c_rows, LANES), lambda b, i: (b * n2 + i, OFF_MISC // LANES)),
                  pl.BlockSpec((CONV_K, 3 * width), lambda b, i: (0, 0)),
                  pl.BlockSpec((8, LANES), lambda b, i: (0, 0))],
        out_specs=[big2] * 5 + [pl.BlockSpec((sub * 8, width), row2)],
        scratch_shapes=[pltpu.VMEM((sub * c_rows + 8, 3 * width), F32)],
        compiler_params=_cparams(("parallel", "arbitrary")),
        name="gdn_prep",
    )(proj, proj, conv_w, gp)
    return pl.pallas_call(
        _gdn_scan_kernel,
        out_shape=jax.ShapeDtypeStruct((m, width), F32),
        grid=(bsz, n2),
        in_specs=[big2] * 5 + [pl.BlockSpec((sub * 8, width), row2),
                               pl.BlockSpec((sub * c_rows, width), lambda b, i: (b * n2 + i, OFF_BZ // width)),
                               pl.BlockSpec((1, B_DIM), lambda b, i: (0, 0))],
        out_specs=big2,
        scratch_shapes=[pltpu.VMEM((B_HEADS, B_DIM, B_DIM), F32)],
        compiler_params=_cparams(("parallel", "arbitrary")),
        name="gdn_scan",
    )(u, w, qd, kdt, attn, gl, proj, norm_g.reshape(1, B_DIM))


def _dsa_prep_kernel(cq_ref, ckv_ref, misc_ref, qg_ref, kvg_ref, kig_ref, kib_ref,
                     wuq_ref, wuk_ref, wqi_ref, qa_ref, qi_ref, ckvn_ref, kpad_ref, wi_ref):
    tm = cq_ref.shape[0]
    ql = (_rms(cq_ref[...]) * qg_ref[...]).astype(BF16)
    qh = _dot(ql, wuq_ref[...]).astype(BF16)
    qa = (_dot(qh, wuk_ref[...]) * (C_HEAD_DIM ** -0.5 * LOG2E)).astype(BF16)
    qi = _dot(ql, wqi_ref[...]).astype(BF16)
    for h in range(C_HEADS):
        qa_ref[h * tm:(h + 1) * tm, :] = qa[:, h * LANES:(h + 1) * LANES]
        qi_ref[h * tm:(h + 1) * tm, :] = qi[:, h * LANES:(h + 1) * LANES]
    ckvn_ref[...] = (_rms(ckv_ref[...]) * kvg_ref[...]).astype(BF16)
    mi = misc_ref[...]
    lane = lax.broadcasted_iota(I32, mi.shape, 1)
    isk = lane < IDX_DIM
    mu = jnp.sum(jnp.where(isk, mi, 0.0), axis=-1, keepdims=True) * (1.0 / IDX_DIM)
    xc = jnp.where(isk, mi - mu, 0.0)
    var = jnp.sum(xc * xc, axis=-1, keepdims=True) * (1.0 / IDX_DIM)
    kn = xc * lax.rsqrt(var + NORM_EPS) * kig_ref[...] + kib_ref[...]
    kpad_ref[...] = jnp.where(isk, kn, 0.0).astype(BF16)
    wi_ref[...] = mi * (IDX_HEADS ** -0.5 * IDX_DIM ** -0.5)


def _float_key(x):
    b = lax.bitcast_convert_type(x, I32)
    return jnp.where(b >= 0, b, b ^ INT_MAX)


def _bit_transpose32(words):
    w = list(words)
    j, m = 16, 0x0000FFFF
    while j:
        k = 0
        while k < 32:
            t = (w[k] ^ lax.shift_right_logical(w[k + j], j)) & m
            w[k] = w[k] ^ t
            w[k + j] = w[k + j] ^ lax.shift_left(t, j)
            k = (k + j + 1) & ~j
        j >>= 1
        m = (m ^ (m << j)) & 0x7FFFFFFF
    return w


def _dsa_main_kernel(qa_ref, qi_ref, wi_ref, kpad_ref, ckv_ref, wuv_ref, o_ref,
                     key_scr, alive_scr, gt_scr, sel_scr, m_scr, l_scr, acc_scr,
                     *, tq, tk, n_sel, idx_bits):
    i = pl.program_id(1)
    nvis = ((i + 1) * tq + tk - 1) // tk
    tiles = tk // LANES
    group = 32 // tiles
    span = 32 * LANES
    ngrp = (nvis + group - 1) // group
    rows = i * tq + lax.broadcasted_iota(I32, (tq, tk), 0)
    col0 = lax.broadcasted_iota(I32, (tq, tk), 1)
    lane = lax.broadcasted_iota(I32, (tq, LANES), 1)
    wi = wi_ref[...]

    def score_block(jb, carry):
        start = pl.multiple_of(jb * tk, tk)
        kblk = kpad_ref[pl.ds(start, tk), :]
        hc = IDX_HEADS // ATT_CHAINS
        ss = [_dot(qi_ref[ch * hc * tq:(ch + 1) * hc * tq, :], kblk, _NT) for ch in range(ATT_CHAINS)]
        acc = None
        for h in range(IDX_HEADS):
            s_h = ss[h // hc][(h % hc) * tq:(h % hc + 1) * tq]
            term = jnp.maximum(s_h, 0.0) * wi[:, MISC_WI + h:MISC_WI + h + 1]
            acc = term if acc is None else acc + term
        key = jnp.where(col0 + start <= rows, _float_key(acc) ^ INT_MIN, 0)
        for tt in range(tiles):
            key_scr[jb * tiles + tt] = key[:, tt * LANES:(tt + 1) * LANES]
        return carry

    _for_blocks(nvis, lambda jb: score_block(jb, 0), per_trip=4)

    def fill_tile(kt, carry):
        key_scr[kt] = jnp.zeros((tq, LANES), I32)
        return carry

    lax.fori_loop(nvis * tiles, ngrp * 32, fill_tile, 0)

    def plane_group(g, carry):
        def plane_rows(rg, carry2):
            r0 = pl.multiple_of(rg * 8, 8)
            planes = _bit_transpose32([key_scr[g * 32 + k, pl.ds(r0, 8), :] for k in range(32)])
            for b in range(32):
                key_scr[g * 32 + b, pl.ds(r0, 8), :] = planes[b]
            return carry2
        lax.fori_loop(0, tq // 8, plane_rows, 0)
        alive_scr[g] = jnp.full((tq, LANES), -1, I32)
        gt_scr[g] = jnp.zeros((tq, LANES), I32)
        return carry

    lax.fori_loop(0, ngrp, plane_group, 0)

    def popcount_rows(mask_fn):
        def body(g, acc):
            return acc + lax.population_count(mask_fn(g))
        acc = lax.fori_loop(0, ngrp, body, jnp.zeros((tq, LANES), I32))
        return jnp.sum(acc.astype(F32), axis=-1, keepdims=True)

    chains = [slice(ch * tq // RADIX_CHAINS, (ch + 1) * tq // RADIX_CHAINS) for ch in range(RADIX_CHAINS)]

    def run_radix(ng):
        def radix_step(it, carry):
            needs, ones = carry
            takes = [o >= n_ for o, n_ in zip(ones, needs)]
            flips = [jnp.where(t_, 0, -1) for t_ in takes]
            nxt = jnp.minimum(it + 1, 31)
            accs = [jnp.zeros((tq // RADIX_CHAINS, LANES), I32) for _ in chains]
            for g in range(ng):
                for ch, r_ in enumerate(chains):
                    a = alive_scr[g, r_, :]
                    p = key_scr[g * 32 + it, r_, :]
                    gt_scr[g, r_, :] = gt_scr[g, r_, :] | (a & p & flips[ch])
                    a = a & (p ^ flips[ch])
                    alive_scr[g, r_, :] = a
                    accs[ch] = accs[ch] + lax.population_count(a & key_scr[g * 32 + nxt, r_, :])
            new_ones = tuple(jnp.sum(acc.astype(F32), axis=-1, keepdims=True) for acc in accs)
            new_needs = tuple(jnp.where(t_, n_, n_ - o) for t_, n_, o in zip(takes, needs, ones))
            return new_needs, new_ones

        def go():
            ones0 = []
            for r_ in chains:
                acc = lax.population_count(key_scr[0, r_, :])
                for g in range(1, ng):
                    acc = acc + lax.population_count(key_scr[g * 32, r_, :])
                ones0.append(jnp.sum(acc.astype(F32), axis=-1, keepdims=True))
            need0 = tuple(jnp.full((tq // RADIX_CHAINS, 1), float(n_sel), F32) for _ in chains)
            needs, _ = lax.fori_loop(0, 32, radix_step, (need0, tuple(ones0)))
            return jnp.concatenate(needs, axis=0)
        return go

    max_grp = key_scr.shape[0] // 32
    need = lax.switch(ngrp - 1, [run_radix(ng) for ng in range(1, max_grp + 1)])
    ties = popcount_rows(lambda g: alive_scr[g])

    def cols_below(g, x):
        nk = jnp.clip((x - g * span - lane + (LANES - 1)) >> 7, 0, 32)
        top = ~lax.shift_right_logical(jnp.full((tq, LANES), -1, I32), jnp.minimum(nk, 31))
        return jnp.where(nk >= 32, -1, top)

    qpos = i * tq + lax.broadcasted_iota(I32, (tq, 1), 0)
    few = qpos < n_sel
    overflow = jnp.where(few, 0.0, ties - need)
    xcut_all = jnp.full((tq, 1), 1 << 30, I32)

    def resolve_ties():
        def cut_step(it, cut):
            cand = cut | lax.shift_left(jnp.int32(1), idx_bits - 1 - it)
            below = popcount_rows(lambda g: alive_scr[g] & cols_below(g, cand))
            return jnp.where(below < need, cand, cut)

        cut = lax.fori_loop(0, idx_bits, cut_step, jnp.zeros((tq, 1), I32))
        return jnp.where(overflow > 0.0, cut + 1, xcut_all)

    xcut = lax.cond(jnp.max(overflow) > 0.0, resolve_ties, lambda: xcut_all)

    def select_group(g, carry):
        chosen = gt_scr[g] | (alive_scr[g] & cols_below(g, xcut))
        sel_scr[g] = jnp.where(few, cols_below(g, qpos + 1), chosen)
        return carry

    lax.fori_loop(0, ngrp, select_group, 0)

    m_scr[...] = jnp.full(m_scr.shape, NEG, F32)
    l_scr[...] = jnp.zeros(l_scr.shape, F32)
    acc_scr[...] = jnp.zeros(acc_scr.shape, F32)

    def attend_block(jb, carry):
        start = pl.multiple_of(jb * tk, tk)
        sel = sel_scr[jb // group]
        k0 = (jb % group) * tiles
        bias = jnp.concatenate(
            [jnp.where(jnp.left_shift(sel, k0 + tt) < 0, 0.0, NEG) for tt in range(tiles)], axis=1)
        ckv = ckv_ref[pl.ds(start, tk), :]
        hc = C_HEADS // ATT_CHAINS
        rs = [slice(ch * hc * tq, (ch + 1) * hc * tq) for ch in range(ATT_CHAINS)]
        ss = [_dot(qa_ref[r_, :], ckv, _NT) for r_ in rs]
        ss = [(s.reshape(hc, tq, tk) + bias[None]).reshape(hc * tq, tk) for s in ss]
        m_prevs = [m_scr[r_, :] for r_ in rs]
        m_news = [jnp.maximum(mp, jnp.max(s, axis=-1, keepdims=True)) for mp, s in zip(m_prevs, ss)]
        alphas = [jnp.exp2(mp - mn) for mp, mn in zip(m_prevs, m_news)]
        ps = [jnp.exp2(s - jnp.tile(mn, (1, tiles))) for s, mn in zip(ss, m_news)]
        pvs = [_dot(p.astype(BF16), ckv) for p in ps]
        for r_, mn, al, p, pv in zip(rs, m_news, alphas, ps, pvs):
            l_scr[r_, :] = al * l_scr[r_, :] + jnp.sum(p, axis=-1, keepdims=True)
            m_scr[r_, :] = mn
            acc_scr[r_, :] = al * acc_scr[r_, :] + pv
        return carry

    _for_blocks(nvis, lambda jb: attend_block(jb, 0), per_trip=4)
    o_all = acc_scr[...] / l_scr[...]
    o_lat = jnp.concatenate([o_all[h * tq:(h + 1) * tq] for h in range(C_HEADS)], axis=1)
    o_ref[...] = _dot(o_lat.astype(BF16), wuv_ref[...])


def _dsa_attention(proj, q_norm_g, kv_norm_g, kidx_g, kidx_b, w_uq, w_qidx, w_uk, w_uv,
                   bsz, t, tq=256, tk=512):
    m = bsz * t
    tq = min(tq, t)
    tk = min(tk, t)
    tm = tq
    hr = C_HEADS * C_KV_LORA
    wuq = w_uq.reshape(C_Q_LORA, C_HEADS * C_HEAD_DIM).astype(BF16)
    eye_h = jnp.eye(C_HEADS, dtype=F32)
    wuk_bd = jnp.einsum('rhd,hg->hdgr', w_uk, eye_h).reshape(C_HEADS * C_HEAD_DIM, hr).astype(BF16)
    wuv_bd = jnp.einsum('rhd,hg->hrgd', w_uv, eye_h).reshape(hr, C_HEADS * C_HEAD_DIM).astype(BF16)
    wqi = jnp.pad(w_qidx, ((0, 0), (0, 0), (0, LANES - IDX_DIM))).reshape(C_Q_LORA, IDX_HEADS * LANES).astype(BF16)
    pad_row = lambda v: jnp.pad(v.astype(F32), (0, LANES - IDX_DIM)).reshape(1, LANES)
    full = lambda shape: pl.BlockSpec(shape, lambda i: (0, 0))
    qa, qi, ckvn, kpad, wi = pl.pallas_call(
        _dsa_prep_kernel,
        out_shape=[jax.ShapeDtypeStruct((m * C_HEADS, LANES), BF16), jax.ShapeDtypeStruct((m * IDX_HEADS, LANES), BF16),
                   jax.ShapeDtypeStruct((m, LANES), BF16), jax.ShapeDtypeStruct((m, LANES), BF16),
                   jax.ShapeDtypeStruct((m, LANES), F32)],
        grid=(m // tm,),
        in_specs=[pl.BlockSpec((tm, C_Q_LORA), lambda i: (i, OFF_CQ // C_Q_LORA)),
                  pl.BlockSpec((tm, LANES), lambda i: (i, OFF_CKV // LANES)),
                  pl.BlockSpec((tm, LANES), lambda i: (i, OFF_MISC // LANES)),
                  full((1, C_Q_LORA)), full((1, LANES)), full((1, LANES)), full((1, LANES)),
                  full(wuq.shape), full(wuk_bd.shape), full(wqi.shape)],
        out_specs=[pl.BlockSpec((C_HEADS * tm, LANES), lambda i: (i, 0)),
                   pl.BlockSpec((IDX_HEADS * tm, LANES), lambda i: (i, 0)),
                   pl.BlockSpec((tm, LANES), lambda i: (i, 0)), pl.BlockSpec((tm, LANES), lambda i: (i, 0)),
                   pl.BlockSpec((tm, LANES), lambda i: (i, 0))],
        compiler_params=_cparams(("parallel",)),
        name="dsa_prep",
    )(proj, proj, proj, q_norm_g.reshape(1, C_Q_LORA), kv_norm_g.reshape(1, LANES),
      pad_row(kidx_g), pad_row(kidx_b), wuq, wuk_bd, wqi)

    nq = t // tq
    n_sel = min(INDEX_TOPK, t // 4)
    idx_bits = max(1, int(math.ceil(math.log2(t))))
    ngrp = -(-t // (32 * LANES))
    blk = lambda rows, w: pl.BlockSpec((rows, w), lambda b, i: (b * nq + i, 0))
    once = pl.Buffered(1)
    seq = lambda w: pl.BlockSpec((t, w), lambda b, i: (b, 0), pipeline_mode=once)
    return pl.pallas_call(
        functools.partial(_dsa_main_kernel, tq=tq, tk=tk, n_sel=n_sel, idx_bits=idx_bits),
        out_shape=jax.ShapeDtypeStruct((m, C_HEADS * C_HEAD_DIM), F32),
        grid=(bsz, nq),
        in_specs=[blk(C_HEADS * tq, LANES), blk(IDX_HEADS * tq, LANES), blk(tq, LANES), seq(LANES), seq(LANES),
                  pl.BlockSpec(wuv_bd.shape, lambda b, i: (0, 0), pipeline_mode=once)],
        out_specs=blk(tq, C_HEADS * C_HEAD_DIM),
        scratch_shapes=[pltpu.VMEM((ngrp * 32, tq, LANES), I32)]
        + [pltpu.VMEM((ngrp, tq, LANES), I32)] * 3
        + [pltpu.VMEM((C_HEADS * tq, LANES), F32)] * 3,
        compiler_params=_cparams(("parallel", "arbitrary")),
        name="dsa_main",
    )(qa, qi, wi, kpad, ckvn, wuv_bd)


def _merge_kernel(x_ref, g0_ref, g1_ref, g2_ref, ya_ref, yb_ref, yc_ref, bg_ref,
                  wa_ref, wb_ref, wc_ref, wo_ref, lg_ref, lb_ref, o_ref, *, alpha):
    bg = bg_ref[...]
    merged = None
    for idx, (g_ref, y_ref, w_ref) in enumerate(
            ((g0_ref, ya_ref, wa_ref), (g1_ref, yb_ref, wb_ref), (g2_ref, yc_ref, wc_ref))):
        gate = _sigmoid(g_ref[...] + bg[idx:idx + 1, :])
        term = gate * _dot(y_ref[...].astype(BF16), w_ref[...])
        merged = term if merged is None else merged + term
    y = alpha * x_ref[...] + _dot(merged.astype(BF16), wo_ref[...])
    o_ref[...] = _layer_norm(y, lg_ref[...], lb_ref[...])


def _merge(x, proj, ya, yb, yc, b_gate, wa, wb, wc, wo, ln_g, ln_b, alpha, tm=512):
    m, d = x.shape
    tm = min(tm, m)
    gate_blk = lambda k: pl.BlockSpec((tm, d), lambda i: (i, OFF_GATES // d + k))
    rowblk = lambda w: pl.BlockSpec((tm, w), lambda i: (i, 0))
    full = lambda a: pl.BlockSpec(a.shape, lambda i: (0, 0))
    bg = b_gate.reshape(3, d)
    ws = [wa.astype(BF16), wb.astype(BF16), wc.astype(BF16), wo.astype(BF16)]
    lg, lb = ln_g.reshape(1, d), ln_b.reshape(1, d)
    return pl.pallas_call(
        functools.partial(_merge_kernel, alpha=alpha),
        out_shape=jax.ShapeDtypeStruct((m, d), F32),
        grid=(m // tm,),
        in_specs=[rowblk(d), gate_blk(0), gate_blk(1), gate_blk(2),
                  rowblk(ya.shape[1]), rowblk(yb.shape[1]), rowblk(yc.shape[1]), full(bg)]
        + [full(w) for w in ws] + [full(lg), full(lb)],
        out_specs=rowblk(d),
        compiler_params=_cparams(("parallel",)),
        name="merge",
    )(x, proj, proj, proj, ya, yb, yc, bg, *ws, lg, lb)


def _ffn_kernel(x_ref, w1_ref, w2_ref, lg_ref, lb_ref, o_ref, acc_scr, *, alpha):
    k = pl.program_id(1)

    @pl.when(k == 0)
    def _():
        acc_scr[...] = jnp.zeros(acc_scr.shape, F32)

    h = jnp.maximum(_dot(x_ref[...].astype(BF16), w1_ref[...]), 0.0)
    acc_scr[...] += _dot((h * h).astype(BF16), w2_ref[...])

    @pl.when(k == pl.num_programs(1) - 1)
    def _():
        o_ref[...] = _layer_norm(alpha * x_ref[...] + acc_scr[...], lg_ref[...], lb_ref[...])


def _ffn(x, w1, w2, ln_g, ln_b, alpha, tm=1024, tf=1024):
    m, d = x.shape
    tm = min(tm, m)
    dff = w1.shape[1]
    return pl.pallas_call(
        functools.partial(_ffn_kernel, alpha=alpha),
        out_shape=jax.ShapeDtypeStruct((m, d), F32),
        grid=(m // tm, dff // tf),
        in_specs=[pl.BlockSpec((tm, d), lambda i, k: (i, 0)),
                  pl.BlockSpec((d, tf), lambda i, k: (0, k)),
                  pl.BlockSpec((tf, d), lambda i, k: (k, 0)),
                  pl.BlockSpec((1, d), lambda i, k: (0, 0)),
                  pl.BlockSpec((1, d), lambda i, k: (0, 0))],
        out_specs=pl.BlockSpec((tm, d), lambda i, k: (i, 0)),
        scratch_shapes=[pltpu.VMEM((tm, d), F32)],
        compiler_params=_cparams(("parallel", "arbitrary")),
        name="ffn",
    )(x, w1.astype(BF16), w2.astype(BF16), ln_g.reshape(1, d), ln_b.reshape(1, d))


def kernel(x, w_in, b_gate, a_lambda, a_subln_g, b_conv_w, b_a_log, b_dt_bias, b_norm_g,
           c_q_norm_g, c_kv_norm_g, c_kidx_g, c_kidx_b, c_w_uq, c_w_qidx, c_w_uk, c_w_uv,
           w_branch_a, w_branch_b, w_branch_c, w_o, ln1_g, ln1_b, w_ff1, w_ff2, ln2_g, ln2_b):
    bsz, t, d = x.shape
    depth = w_in.shape[0]
    alpha = (2 * depth) ** 0.25
    xf = x.reshape(bsz * t, d)
    for l in range(depth):
        lam_init = 0.8 - 0.6 * math.exp(-0.3 * l)
        w_main, w_kv = _pack_w_in(w_in[l])
        proj = _matmul(xf, w_main, F32, tm=min(1024, bsz * t), tn=1024)
        proj_kv = _matmul(xf, w_kv, BF16, tm=min(1024, bsz * t), tn=1024)
        y_a = _diff_attention(proj, proj_kv, a_lambda[l], a_subln_g[l], bsz, t, lam_init)
        y_b = _gated_deltanet(proj, b_conv_w[l], b_a_log[l], b_dt_bias[l], b_norm_g[l], bsz, t)
        y_c = _dsa_attention(proj, c_q_norm_g[l], c_kv_norm_g[l], c_kidx_g[l], c_kidx_b[l],
                             c_w_uq[l], c_w_qidx[l], c_w_uk[l], c_w_uv[l], bsz, t)
        xf = _merge(xf, proj, y_a, y_b, y_c, b_gate[l], w_branch_a[l], w_branch_b[l], w_branch_c[l],
                    w_o[l], ln1_g[l], ln1_b[l], alpha)
        xf = _ffn(xf, w_ff1[l], w_ff2[l], ln2_g[l], ln2_b[l], alpha)
    return xf.reshape(bsz, t, d)
```

```python
import functools
import math

import jax
import jax.numpy as jnp
from jax import lax
from jax.experimental import pallas as pl
from jax.experimental.pallas import tpu as pltpu

F32 = jnp.float32
BF16 = jnp.bfloat16
I32 = jnp.int32

LANES = 128
NEG = -1e30
LOG2E = 1.4426950408889634
INT_MIN = -2 ** 31
INT_MAX = 2 ** 31 - 1
VMEM_LIMIT = 56 * 1024 * 1024

A_HEADS = 4
A_QK_DIM = 64
B_HEADS = 4
B_DIM = 128
CONV_K = 4
GDN_CHUNK = 128
GDN_SUB = 4
C_HEADS = 8
C_HEAD_DIM = 64
C_Q_LORA = 256
C_KV_LORA = 128
IDX_HEADS = 8
IDX_DIM = 64
INDEX_TOPK = 256
NORM_EPS = 1e-6
RADIX_CHAINS = 2
ATT_CHAINS = 2

OFF_BQKV, OFF_AQ, OFF_BZ = 0, 1536, 2048
OFF_CQ, OFF_CKV, OFF_MISC, OFF_GATES = 2560, 2816, 2944, 3072
OFF_AK, OFF_AV = 0, 512
MISC_WI, MISC_A, MISC_B = 64, 72, 76

_NT = (((1,), (1,)), ((), ()))
_NN = (((1,), (0,)), ((), ()))


def _cparams(sem):
    return pltpu.CompilerParams(dimension_semantics=sem, vmem_limit_bytes=VMEM_LIMIT)


def _dot(a, b, dims=_NN):
    return lax.dot_general(a, b, dims, preferred_element_type=F32)


def _for_blocks(n, body, per_trip=2):
    def trip(jj, carry):
        for u in range(per_trip):
            body(per_trip * jj + u)
        return carry

    lax.fori_loop(0, n // per_trip, trip, 0)
    done = (n // per_trip) * per_trip
    step = per_trip // 2
    while step:
        has = (n - done) >= step

        @pl.when(has)
        def _(done=done, step=step):
            for u in range(step):
                body(done + u)
        done = done + jnp.where(has, step, 0)
        step //= 2


def _layer_norm(x, g, b):
    mu = jnp.mean(x, axis=-1, keepdims=True)
    xc = x - mu
    var = jnp.mean(xc * xc, axis=-1, keepdims=True)
    return xc * lax.rsqrt(var + NORM_EPS) * g + b


def _rms(x):
    return x * lax.rsqrt(jnp.mean(x * x, axis=-1, keepdims=True) + NORM_EPS)


def _sigmoid(x):
    return 1.0 / (1.0 + jnp.exp(-x))


def _mm_kernel(x_ref, w_ref, o_ref):
    o_ref[...] = _dot(x_ref[...].astype(BF16), w_ref[...]).astype(o_ref.dtype)


def _matmul(x, w, out_dtype, tm, tn):
    m, k = x.shape
    n = w.shape[1]
    return pl.pallas_call(
        _mm_kernel,
        out_shape=jax.ShapeDtypeStruct((m, n), out_dtype),
        grid=(m // tm, n // tn),
        in_specs=[pl.BlockSpec((tm, k), lambda i, j: (i, 0)),
                  pl.BlockSpec((k, tn), lambda i, j: (0, j))],
        out_specs=pl.BlockSpec((tm, tn), lambda i, j: (i, j)),
        compiler_params=_cparams(("parallel", "parallel")),
        name="in_proj",
    )(x, w)


def _pack_w_in(w):
    d = w.shape[0]
    misc = jnp.concatenate(
        [w[:, 3976:4040], w[:, 4040:4048], w[:, 3584:3588], w[:, 3588:3592],
         jnp.zeros((d, LANES - 80), w.dtype)], axis=1)
    packed = jnp.concatenate(
        [w[:, 1536:3072], w[:, 0:512], w[:, 3072:3584], w[:, 3592:3848], w[:, 3848:3976], misc,
         w[:, 4048:7120]], axis=1)
    return packed.astype(BF16), w[:, 512:1536].astype(BF16)


def _diffattn_kernel(lam_ref, g_ref, q_ref, k_ref, v_ref, o_ref, m_scr, l_scr, acc_scr, *, tq, tk, lam_init):
    i = pl.program_id(2)
    tiles = tk // LANES
    ratio = tq // tk
    q = q_ref[...] * (A_QK_DIM ** -0.5 * LOG2E)
    lane = lax.broadcasted_iota(I32, q.shape, 1)
    q2 = jnp.concatenate([jnp.where(lane < A_QK_DIM, q, 0.0),
                          jnp.where(lane >= A_QK_DIM, q, 0.0)], axis=0).astype(BF16)
    m_scr[...] = jnp.full(m_scr.shape, NEG, F32)
    l_scr[...] = jnp.zeros(l_scr.shape, F32)
    acc_scr[...] = jnp.zeros(acc_scr.shape, F32)

    def block(j, diagonal):
        start = pl.multiple_of(j * tk, tk)
        kb = k_ref[pl.ds(start, tk), :]
        vb = v_ref[pl.ds(start, tk), :]
        rs = [slice(0, tq), slice(tq, 2 * tq)]
        ss = [_dot(q2[r_], kb, _NT) for r_ in rs]
        if diagonal:
            r = lax.broadcasted_iota(I32, (tq, tk), 0)
            cc = lax.broadcasted_iota(I32, (tq, tk), 1)
            bias = jnp.where(cc + (j - i * ratio) * tk <= r, 0.0, NEG)
            ss = [s + bias for s in ss]
        m_prevs = [m_scr[r_, :] for r_ in rs]
        m_news = [jnp.maximum(mp, jnp.max(s, axis=-1, keepdims=True)) for mp, s in zip(m_prevs, ss)]
        alphas = [jnp.exp2(mp - mn) for mp, mn in zip(m_prevs, m_news)]
        ps = [jnp.exp2(s - jnp.tile(mn, (1, tiles))) for s, mn in zip(ss, m_news)]
        pvs = [_dot(p.astype(BF16), vb) for p in ps]
        for r_, mn, al, p, pv in zip(rs, m_news, alphas, ps, pvs):
            l_scr[r_, :] = al * l_scr[r_, :] + jnp.sum(p, axis=-1, keepdims=True)
            m_scr[r_, :] = mn
            acc_scr[r_, :] = al * acc_scr[r_, :] + pv

    _for_blocks(i * ratio, lambda j: block(j, False), per_trip=4)
    for d in range(ratio):
        block(i * ratio + d, True)

    lp = lam_ref[...]
    s1 = jnp.sum(lp[0:1, :] * lp[1:2, :], axis=-1, keepdims=True)
    s2 = jnp.sum(lp[2:3, :] * lp[3:4, :], axis=-1, keepdims=True)
    lam = jnp.exp(s1) - jnp.exp(s2) + lam_init
    o_all = acc_scr[...] / l_scr[...]
    o = o_all[0:tq] - lam * o_all[tq:2 * tq]
    o_ref[...] = _rms(o) * g_ref[...] * (1.0 - lam_init)


def _diff_attention(proj, proj_kv, a_lambda, subln_g, bsz, t, lam_init, tq=512, tk=512):
    tq = min(tq, t)
    tk = min(tk, tq)
    nq = t // tq
    qo, ko, vo = OFF_AQ // LANES, OFF_AK // LANES, OFF_AV // LANES
    return pl.pallas_call(
        functools.partial(_diffattn_kernel, tq=tq, tk=tk, lam_init=lam_init),
        out_shape=jax.ShapeDtypeStruct((bsz * t, A_HEADS * LANES), F32),
        grid=(bsz, A_HEADS, nq),
        in_specs=[
            pl.BlockSpec((4, A_QK_DIM), lambda b, h, i: (0, 0)),
            pl.BlockSpec((1, LANES), lambda b, h, i: (0, 0)),
            pl.BlockSpec((tq, LANES), lambda b, h, i: (b * nq + i, qo + h)),
            pl.BlockSpec((t, LANES), lambda b, h, i: (b, ko + h)),
            pl.BlockSpec((t, LANES), lambda b, h, i: (b, vo + h)),
        ],
        out_specs=pl.BlockSpec((tq, LANES), lambda b, h, i: (b * nq + i, h)),
        scratch_shapes=[pltpu.VMEM((2 * tq, LANES), F32)] * 3,
        compiler_params=_cparams(("parallel", "parallel", "arbitrary")),
        name="diff_attention",
    )(a_lambda, subln_g.reshape(1, LANES), proj, proj_kv, proj_kv)


def _gdn_kernel(qkv_ref, misc_ref, cw_ref, gp_ref, z_ref, g_ref, o_ref, xs_scr, s_scr):
    c_rows = GDN_CHUNK
    rows = c_rows * GDN_SUB
    n = pl.program_id(1)

    @pl.when(n == 0)
    def _():
        xs_scr[0:8, :] = jnp.zeros((8, xs_scr.shape[1]), F32)

    xs_scr[8:8 + rows, :] = qkv_ref[...]
    cw = cw_ref[...]
    y = cw[0:1, :] * xs_scr[5:5 + rows, :]
    for jj in range(1, CONV_K):
        y = y + cw[jj:jj + 1, :] * xs_scr[5 + jj:5 + jj + rows, :]
    xs_scr[0:8, :] = xs_scr[rows:rows + 8, :]
    y = y * _sigmoid(y)

    mi = misc_ref[...]
    gp = gp_ref[...]
    xg = mi + gp[1:2, :]
    softplus = jnp.maximum(xg, 0.0) + jnp.log(1.0 + jnp.exp(-jnp.abs(xg)))
    g_all = -jnp.exp(gp[0:1, :]) * softplus
    beta_all = _sigmoid(mi)

    r = lax.broadcasted_iota(I32, (c_rows, c_rows), 0)
    c = lax.broadcasted_iota(I32, (c_rows, c_rows), 1)
    incl = r >= c
    strict = r > c
    eye = (r == c).astype(F32)
    ones_l = incl.astype(BF16)
    g_hi = g_all.astype(BF16)
    g_r1 = g_all - g_hi.astype(F32)
    g_mid = g_r1.astype(BF16)
    g_lo = (g_r1 - g_mid.astype(F32)).astype(BF16)
    subs = [slice(sc * c_rows, (sc + 1) * c_rows) for sc in range(GDN_SUB)]
    gcs = [_dot(ones_l, g_hi[rs_]) + (_dot(ones_l, g_mid[rs_]) + _dot(ones_l, g_lo[rs_]))
           for rs_ in subs]
    gcts = [gc.T for gc in gcs]

    heads = range(GDN_SUB * B_HEADS)
    qs, ks, vs, gcols, bcols, decays, kbs, khbs = [], [], [], [], [], [], [], []
    for inst in heads:
        sc, h = divmod(inst, B_HEADS)
        rs_ = subs[sc]
        qh = y[rs_, h * B_DIM:(h + 1) * B_DIM]
        kh = y[rs_, 512 + h * B_DIM:512 + (h + 1) * B_DIM]
        qs.append(qh * lax.rsqrt(jnp.sum(qh * qh, axis=-1, keepdims=True) + NORM_EPS) * (B_DIM ** -0.5))
        kh = kh * lax.rsqrt(jnp.sum(kh * kh, axis=-1, keepdims=True) + NORM_EPS)
        ks.append(kh)
        vs.append(y[rs_, 1024 + h * B_DIM:1024 + (h + 1) * B_DIM])
        gcol = gcs[sc][:, MISC_A + h:MISC_A + h + 1]
        grow = gcts[sc][MISC_A + h:MISC_A + h + 1, :]
        gcols.append(gcol)
        bcols.append(beta_all[rs_, MISC_B + h:MISC_B + h + 1])
        decays.append(jnp.where(incl, jnp.exp(jnp.where(incl, gcol - grow, 0.0)), 0.0))
        kbs.append(kh * bcols[inst])
        khbs.append(kh.astype(BF16))
    lows = [jnp.where(strict, _dot(kbs[h].astype(BF16), khbs[h], _NT) * decays[h], 0.0) for h in heads]
    attns = [_dot(qs[h].astype(BF16), khbs[h], _NT) * decays[h] for h in heads]

    same16 = (r >> 4) == (c >> 4)
    pws = [jnp.where(same16, lows[h], 0.0) for h in heads]
    invs = [eye - pws[h] for h in heads]
    for _ in range(3):
        pwbs = [pws[h].astype(BF16) for h in heads]
        pws = [_dot(pwbs[h], pwbs[h]) for h in heads]
        invs = [invs[h] + _dot(invs[h].astype(BF16), pws[h].astype(BF16)) for h in heads]
    for sh in (4, 5, 6):
        pair = ((r >> (sh + 1)) == (c >> (sh + 1))) & ((r >> sh) != (c >> sh))
        invbs = [invs[h].astype(BF16) for h in heads]
        mids = [_dot(jnp.where(pair, lows[h], 0.0).astype(BF16), invbs[h]).astype(BF16) for h in heads]
        invs = [invs[h] - _dot(invbs[h], mids[h]) for h in heads]

    egs = [jnp.exp(gcols[h]) for h in heads]
    sols = [_dot(invs[h].astype(BF16),
                 jnp.concatenate([vs[h] * bcols[h], kbs[h] * egs[h]], axis=1).astype(BF16)) for h in heads]
    glasts = [gcs[inst // B_HEADS][c_rows - 1:c_rows, MISC_A + inst % B_HEADS:MISC_A + inst % B_HEADS + 1]
              for inst in heads]
    qds = [(qs[h] * egs[h]).astype(BF16) for h in heads]
    kdts = [(ks[h] * jnp.exp(glasts[h] - gcols[h])).T.astype(BF16) for h in heads]
    attnbs = [attns[h].astype(BF16) for h in heads]

    @pl.when(n == 0)
    def _():
        s_scr[...] = jnp.zeros(s_scr.shape, F32)

    g = g_ref[...]
    hs = range(B_HEADS)
    ss = [s_scr[h] for h in hs]
    for sc in range(GDN_SUB):
        rs_ = subs[sc]
        ids = [sc * B_HEADS + h for h in hs]
        sbs = [ss[h].astype(BF16) for h in hs]
        wss = [_dot(sols[i][:, B_DIM:].astype(BF16), sbs[h]) for h, i in zip(hs, ids)]
        qss = [_dot(qds[i], sbs[h]) for h, i in zip(hs, ids)]
        vbs = [(sols[i][:, :B_DIM] - wss[h]).astype(BF16) for h, i in zip(hs, ids)]
        os_ = [qss[h] + _dot(attnbs[i], vbs[h]) for h, i in zip(hs, ids)]
        kvs = [_dot(kdts[i], vbs[h]) for h, i in zip(hs, ids)]
        ss = [ss[h] * jnp.exp(glasts[i]) + kvs[h] for h, i in zip(hs, ids)]
        for h in hs:
            sl = slice(h * B_DIM, (h + 1) * B_DIM)
            z = z_ref[rs_, sl]
            o_ref[rs_, sl] = _rms(os_[h]) * g * (z * _sigmoid(z))
    for h in hs:
        s_scr[h] = ss[h]


def _gated_deltanet(proj, conv_w, a_log, dt_bias, norm_g, bsz, t):
    m = bsz * t
    c_rows = GDN_CHUNK
    n = t // c_rows
    width = B_HEADS * B_DIM
    gp = jnp.zeros((8, LANES), F32)
    gp = gp.at[0, MISC_A:MISC_A + B_HEADS].set(a_log.astype(F32))
    gp = gp.at[1, MISC_A:MISC_A + B_HEADS].set(dt_bias.astype(F32))
    sub = min(GDN_SUB, n)
    assert sub == GDN_SUB and n % sub == 0
    n2 = n // sub
    return pl.pallas_call(
        _gdn_kernel,
        out_shape=jax.ShapeDtypeStruct((m, width), F32),
        grid=(bsz, n2),
        in_specs=[pl.BlockSpec((sub * c_rows, 3 * width), lambda b, i: (b * n2 + i, OFF_BQKV // (3 * width))),
                  pl.BlockSpec((sub * c_rows, LANES), lambda b, i: (b * n2 + i, OFF_MISC // LANES)),
                  pl.BlockSpec((CONV_K, 3 * width), lambda b, i: (0, 0)),
                  pl.BlockSpec((8, LANES), lambda b, i: (0, 0)),
                  pl.BlockSpec((sub * c_rows, width), lambda b, i: (b * n2 + i, OFF_BZ // width)),
                  pl.BlockSpec((1, B_DIM), lambda b, i: (0, 0))],
        out_specs=pl.BlockSpec((sub * c_rows, width), lambda b, i: (b * n2 + i, 0)),
        scratch_shapes=[pltpu.VMEM((sub * c_rows + 8, 3 * width), F32),
                        pltpu.VMEM((B_HEADS, B_DIM, B_DIM), F32)],
        compiler_params=_cparams(("parallel", "arbitrary")),
        name="gdn",
    )(proj, proj, conv_w, gp, proj, norm_g.reshape(1, B_DIM))


def _dsa_prep_kernel(cq_ref, ckv_ref, misc_ref, qg_ref, kvg_ref, kig_ref, kib_ref,
                     wuq_ref, wuk_ref, wqi_ref, qa_ref, qi_ref, ckvn_ref, kpad_ref, wi_ref):
    tm = cq_ref.shape[0]
    ql = (_rms(cq_ref[...]) * qg_ref[...]).astype(BF16)
    qh = _dot(ql, wuq_ref[...]).astype(BF16)
    qa = (_dot(qh, wuk_ref[...]) * (C_HEAD_DIM ** -0.5 * LOG2E)).astype(BF16)
    qi = _dot(ql, wqi_ref[...]).astype(BF16)
    for h in range(C_HEADS):
        qa_ref[h * tm:(h + 1) * tm, :] = qa[:, h * LANES:(h + 1) * LANES]
        qi_ref[h * tm:(h + 1) * tm, :] = qi[:, h * LANES:(h + 1) * LANES]
    ckvn_ref[...] = (_rms(ckv_ref[...]) * kvg_ref[...]).astype(BF16)
    mi = misc_ref[...]
    lane = lax.broadcasted_iota(I32, mi.shape, 1)
    isk = lane < IDX_DIM
    mu = jnp.sum(jnp.where(isk, mi, 0.0), axis=-1, keepdims=True) * (1.0 / IDX_DIM)
    xc = jnp.where(isk, mi - mu, 0.0)
    var = jnp.sum(xc * xc, axis=-1, keepdims=True) * (1.0 / IDX_DIM)
    kn = xc * lax.rsqrt(var + NORM_EPS) * kig_ref[...] + kib_ref[...]
    kpad_ref[...] = jnp.where(isk, kn, 0.0).astype(BF16)
    wi_ref[...] = mi * (IDX_HEADS ** -0.5 * IDX_DIM ** -0.5)


def _float_key(x):
    b = lax.bitcast_convert_type(x, I32)
    return jnp.where(b >= 0, b, b ^ INT_MAX)


def _bit_transpose32(words):
    w = list(words)
    j, m = 16, 0x0000FFFF
    while j:
        k = 0
        while k < 32:
            t = (w[k] ^ lax.shift_right_logical(w[k + j], j)) & m
            w[k] = w[k] ^ t
            w[k + j] = w[k + j] ^ lax.shift_left(t, j)
            k = (k + j + 1) & ~j
        j >>= 1
        m = (m ^ (m << j)) & 0x7FFFFFFF
    return w


def _dsa_main_kernel(qa_ref, qi_ref, wi_ref, kpad_ref, ckv_ref, wuv_ref, o_ref,
                     key_scr, alive_scr, gt_scr, sel_scr, m_scr, l_scr, acc_scr,
                     *, tq, tk, n_sel, idx_bits):
    i = pl.program_id(1)
    nvis = ((i + 1) * tq + tk - 1) // tk
    tiles = tk // LANES
    group = 32 // tiles
    span = 32 * LANES
    ngrp = (nvis + group - 1) // group
    rows = i * tq + lax.broadcasted_iota(I32, (tq, tk), 0)
    col0 = lax.broadcasted_iota(I32, (tq, tk), 1)
    lane = lax.broadcasted_iota(I32, (tq, LANES), 1)
    wi = wi_ref[...]

    def score_block(jb, carry):
        start = pl.multiple_of(jb * tk, tk)
        kblk = kpad_ref[pl.ds(start, tk), :]
        hc = IDX_HEADS // ATT_CHAINS
        ss = [_dot(qi_ref[ch * hc * tq:(ch + 1) * hc * tq, :], kblk, _NT) for ch in range(ATT_CHAINS)]
        acc = None
        for h in range(IDX_HEADS):
            s_h = ss[h // hc][(h % hc) * tq:(h % hc + 1) * tq]
            term = jnp.maximum(s_h, 0.0) * wi[:, MISC_WI + h:MISC_WI + h + 1]
            acc = term if acc is None else acc + term
        key = jnp.where(col0 + start <= rows, _float_key(acc) ^ INT_MIN, 0)
        for tt in range(tiles):
            key_scr[jb * tiles + tt] = key[:, tt * LANES:(tt + 1) * LANES]
        return carry

    _for_blocks(nvis, lambda jb: score_block(jb, 0), per_trip=4)

    def fill_tile(kt, carry):
        key_scr[kt] = jnp.zeros((tq, LANES), I32)
        return carry

    lax.fori_loop(nvis * tiles, ngrp * 32, fill_tile, 0)

    def plane_group(g, carry):
        def plane_rows(rg, carry2):
            r0 = pl.multiple_of(rg * 8, 8)
            planes = _bit_transpose32([key_scr[g * 32 + k, pl.ds(r0, 8), :] for k in range(32)])
            for b in range(32):
                key_scr[g * 32 + b, pl.ds(r0, 8), :] = planes[b]
            return carry2
        lax.fori_loop(0, tq // 8, plane_rows, 0)
        alive_scr[g] = jnp.full((tq, LANES), -1, I32)
        gt_scr[g] = jnp.zeros((tq, LANES), I32)
        return carry

    lax.fori_loop(0, ngrp, plane_group, 0)

    def popcount_rows(mask_fn):
        def body(g, acc):
            return acc + lax.population_count(mask_fn(g))
        acc = lax.fori_loop(0, ngrp, body, jnp.zeros((tq, LANES), I32))
        return jnp.sum(acc.astype(F32), axis=-1, keepdims=True)

    chains = [slice(ch * tq // RADIX_CHAINS, (ch + 1) * tq // RADIX_CHAINS) for ch in range(RADIX_CHAINS)]

    def run_radix(ng):
        def radix_step(it, carry):
            needs, ones = carry
            takes = [o >= n_ for o, n_ in zip(ones, needs)]
            flips = [jnp.where(t_, 0, -1) for t_ in takes]
            nxt = jnp.minimum(it + 1, 31)
            accs = [jnp.zeros((tq // RADIX_CHAINS, LANES), I32) for _ in chains]
            for g in range(ng):
                for ch, r_ in enumerate(chains):
                    a = alive_scr[g, r_, :]
                    p = key_scr[g * 32 + it, r_, :]
                    gt_scr[g, r_, :] = gt_scr[g, r_, :] | (a & p & flips[ch])
                    a = a & (p ^ flips[ch])
                    alive_scr[g, r_, :] = a
                    accs[ch] = accs[ch] + lax.population_count(a & key_scr[g * 32 + nxt, r_, :])
            new_ones = tuple(jnp.sum(acc.astype(F32), axis=-1, keepdims=True) for acc in accs)
            new_needs = tuple(jnp.where(t_, n_, n_ - o) for t_, n_, o in zip(takes, needs, ones))
            return new_needs, new_ones

        def go():
            ones0 = []
            for r_ in chains:
                acc = lax.population_count(key_scr[0, r_, :])
                for g in range(1, ng):
                    acc = acc + lax.population_count(key_scr[g * 32, r_, :])
                ones0.append(jnp.sum(acc.astype(F32), axis=-1, keepdims=True))
            need0 = tuple(jnp.full((tq // RADIX_CHAINS, 1), float(n_sel), F32) for _ in chains)
            needs, _ = lax.fori_loop(0, 32, radix_step, (need0, tuple(ones0)))
            return jnp.concatenate(needs, axis=0)
        return go

    max_grp = key_scr.shape[0] // 32
    need = lax.switch(ngrp - 1, [run_radix(ng) for ng in range(1, max_grp + 1)])
    ties = popcount_rows(lambda g: alive_scr[g])

    def cols_below(g, x):
        nk = jnp.clip((x - g * span - lane + (LANES - 1)) >> 7, 0, 32)
        top = ~lax.shift_right_logical(jnp.full((tq, LANES), -1, I32), jnp.minimum(nk, 31))
        return jnp.where(nk >= 32, -1, top)

    qpos = i * tq + lax.broadcasted_iota(I32, (tq, 1), 0)
    few = qpos < n_sel
    overflow = jnp.where(few, 0.0, ties - need)
    xcut_all = jnp.full((tq, 1), 1 << 30, I32)

    def resolve_ties():
        def cut_step(it, cut):
            cand = cut | lax.shift_left(jnp.int32(1), idx_bits - 1 - it)
            below = popcount_rows(lambda g: alive_scr[g] & cols_below(g, cand))
            return jnp.where(below < need, cand, cut)

        cut = lax.fori_loop(0, idx_bits, cut_step, jnp.zeros((tq, 1), I32))
        return jnp.where(overflow > 0.0, cut + 1, xcut_all)

    xcut = lax.cond(jnp.max(overflow) > 0.0, resolve_ties, lambda: xcut_all)

    def select_group(g, carry):
        chosen = gt_scr[g] | (alive_scr[g] & cols_below(g, xcut))
        sel_scr[g] = jnp.where(few, cols_below(g, qpos + 1), chosen)
        return carry

    lax.fori_loop(0, ngrp, select_group, 0)

    m_scr[...] = jnp.full(m_scr.shape, NEG, F32)
    l_scr[...] = jnp.zeros(l_scr.shape, F32)
    acc_scr[...] = jnp.zeros(acc_scr.shape, F32)

    def attend_block(jb, carry):
        start = pl.multiple_of(jb * tk, tk)
        sel = sel_scr[jb // group]
        k0 = (jb % group) * tiles
        bias = jnp.concatenate(
            [jnp.where(jnp.left_shift(sel, k0 + tt) < 0, 0.0, NEG) for tt in range(tiles)], axis=1)
        ckv = ckv_ref[pl.ds(start, tk), :]
        hc = C_HEADS // ATT_CHAINS
        rs = [slice(ch * hc * tq, (ch + 1) * hc * tq) for ch in range(ATT_CHAINS)]
        ss = [_dot(qa_ref[r_, :], ckv, _NT) for r_ in rs]
        ss = [(s.reshape(hc, tq, tk) + bias[None]).reshape(hc * tq, tk) for s in ss]
        m_prevs = [m_scr[r_, :] for r_ in rs]
        m_news = [jnp.maximum(mp, jnp.max(s, axis=-1, keepdims=True)) for mp, s in zip(m_prevs, ss)]
        alphas = [jnp.exp2(mp - mn) for mp, mn in zip(m_prevs, m_news)]
        ps = [jnp.exp2(s - jnp.tile(mn, (1, tiles))) for s, mn in zip(ss, m_news)]
        pvs = [_dot(p.astype(BF16), ckv) for p in ps]
        for r_, mn, al, p, pv in zip(rs, m_news, alphas, ps, pvs):
            l_scr[r_, :] = al * l_scr[r_, :] + jnp.sum(p, axis=-1, keepdims=True)
            m_scr[r_, :] = mn
            acc_scr[r_, :] = al * acc_scr[r_, :] + pv
        return carry

    _for_blocks(nvis, lambda jb: attend_block(jb, 0), per_trip=4)
    o_all = acc_scr[...] / l_scr[...]
    o_lat = jnp.concatenate([o_all[h * tq:(h + 1) * tq] for h in range(C_HEADS)], axis=1)
    o_ref[...] = _dot(o_lat.astype(BF16), wuv_ref[...])


def _dsa_attention(proj, q_norm_g, kv_norm_g, kidx_g, kidx_b, w_uq, w_qidx, w_uk, w_uv,
                   bsz, t, tq=256, tk=512):
    m = bsz * t
    tq = min(tq, t)
    tk = min(tk, t)
    tm = tq
    hr = C_HEADS * C_KV_LORA
    wuq = w_uq.reshape(C_Q_LORA, C_HEADS * C_HEAD_DIM).astype(BF16)
    eye_h = jnp.eye(C_HEADS, dtype=F32)
    wuk_bd = jnp.einsum('rhd,hg->hdgr', w_uk, eye_h).reshape(C_HEADS * C_HEAD_DIM, hr).astype(BF16)
    wuv_bd = jnp.einsum('rhd,hg->hrgd', w_uv, eye_h).reshape(hr, C_HEADS * C_HEAD_DIM).astype(BF16)
    wqi = jnp.pad(w_qidx, ((0, 0), (0, 0), (0, LANES - IDX_DIM))).reshape(C_Q_LORA, IDX_HEADS * LANES).astype(BF16)
    pad_row = lambda v: jnp.pad(v.astype(F32), (0, LANES - IDX_DIM)).reshape(1, LANES)
    full = lambda shape: pl.BlockSpec(shape, lambda i: (0, 0))
    qa, qi, ckvn, kpad, wi = pl.pallas_call(
        _dsa_prep_kernel,
        out_shape=[jax.ShapeDtypeStruct((m * C_HEADS, LANES), BF16), jax.ShapeDtypeStruct((m * IDX_HEADS, LANES), BF16),
                   jax.ShapeDtypeStruct((m, LANES), BF16), jax.ShapeDtypeStruct((m, LANES), BF16),
                   jax.ShapeDtypeStruct((m, LANES), F32)],
        grid=(m // tm,),
        in_specs=[pl.BlockSpec((tm, C_Q_LORA), lambda i: (i, OFF_CQ // C_Q_LORA)),
                  pl.BlockSpec((tm, LANES), lambda i: (i, OFF_CKV // LANES)),
                  pl.BlockSpec((tm, LANES), lambda i: (i, OFF_MISC // LANES)),
                  full((1, C_Q_LORA)), full((1, LANES)), full((1, LANES)), full((1, LANES)),
                  full(wuq.shape), full(wuk_bd.shape), full(wqi.shape)],
        out_specs=[pl.BlockSpec((C_HEADS * tm, LANES), lambda i: (i, 0)),
                   pl.BlockSpec((IDX_HEADS * tm, LANES), lambda i: (i, 0)),
                   pl.BlockSpec((tm, LANES), lambda i: (i, 0)), pl.BlockSpec((tm, LANES), lambda i: (i, 0)),
                   pl.BlockSpec((tm, LANES), lambda i: (i, 0))],
        compiler_params=_cparams(("parallel",)),
        name="dsa_prep",
    )(proj, proj, proj, q_norm_g.reshape(1, C_Q_LORA), kv_norm_g.reshape(1, LANES),
      pad_row(kidx_g), pad_row(kidx_b), wuq, wuk_bd, wqi)

    nq = t // tq
    n_sel = min(INDEX_TOPK, t // 4)
    idx_bits = max(1, int(math.ceil(math.log2(t))))
    ngrp = -(-t // (32 * LANES))
    blk = lambda rows, w: pl.BlockSpec((rows, w), lambda b, i: (b * nq + i, 0))
    once = pl.Buffered(1)
    seq = lambda w: pl.BlockSpec((t, w), lambda b, i: (b, 0), pipeline_mode=once)
    return pl.pallas_call(
        functools.partial(_dsa_main_kernel, tq=tq, tk=tk, n_sel=n_sel, idx_bits=idx_bits),
        out_shape=jax.ShapeDtypeStruct((m, C_HEADS * C_HEAD_DIM), F32),
        grid=(bsz, nq),
        in_specs=[blk(C_HEADS * tq, LANES), blk(IDX_HEADS * tq, LANES), blk(tq, LANES), seq(LANES), seq(LANES),
                  pl.BlockSpec(wuv_bd.shape, lambda b, i: (0, 0), pipeline_mode=once)],
        out_specs=blk(tq, C_HEADS * C_HEAD_DIM),
        scratch_shapes=[pltpu.VMEM((ngrp * 32, tq, LANES), I32)]
        + [pltpu.VMEM((ngrp, tq, LANES), I32)] * 3
        + [pltpu.VMEM((C_HEADS * tq, LANES), F32)] * 3,
        compiler_params=_cparams(("parallel", "arbitrary")),
        name="dsa_main",
    )(qa, qi, wi, kpad, ckvn, wuv_bd)


def _merge_kernel(x_ref, g0_ref, g1_ref, g2_ref, ya_ref, yb_ref, yc_ref, bg_ref,
                  wa_ref, wb_ref, wc_ref, wo_ref, lg_ref, lb_ref, o_ref, *, alpha):
    bg = bg_ref[...]
    merged = None
    for idx, (g_ref, y_ref, w_ref) in enumerate(
            ((g0_ref, ya_ref, wa_ref), (g1_ref, yb_ref, wb_ref), (g2_ref, yc_ref, wc_ref))):
        gate = _sigmoid(g_ref[...] + bg[idx:idx + 1, :])
        term = gate * _dot(y_ref[...].astype(BF16), w_ref[...])
        merged = term if merged is None else merged + term
    y = alpha * x_ref[...] + _dot(merged.astype(BF16), wo_ref[...])
    o_ref[...] = _layer_norm(y, lg_ref[...], lb_ref[...])


def _merge(x, proj, ya, yb, yc, b_gate, wa, wb, wc, wo, ln_g, ln_b, alpha, tm=512):
    m, d = x.shape
    tm = min(tm, m)
    gate_blk = lambda k: pl.BlockSpec((tm, d), lambda i: (i, OFF_GATES // d + k))
    rowblk = lambda w: pl.BlockSpec((tm, w), lambda i: (i, 0))
    full = lambda a: pl.BlockSpec(a.shape, lambda i: (0, 0))
    bg = b_gate.reshape(3, d)
    ws = [wa.astype(BF16), wb.astype(BF16), wc.astype(BF16), wo.astype(BF16)]
    lg, lb = ln_g.reshape(1, d), ln_b.reshape(1, d)
    return pl.pallas_call(
        functools.partial(_merge_kernel, alpha=alpha),
        out_shape=jax.ShapeDtypeStruct((m, d), F32),
        grid=(m // tm,),
        in_specs=[rowblk(d), gate_blk(0), gate_blk(1), gate_blk(2),
                  rowblk(ya.shape[1]), rowblk(yb.shape[1]), rowblk(yc.shape[1]), full(bg)]
        + [full(w) for w in ws] + [full(lg), full(lb)],
        out_specs=rowblk(d),
        compiler_params=_cparams(("parallel",)),
        name="merge",
    )(x, proj, proj, proj, ya, yb, yc, bg, *ws, lg, lb)


def _ffn_kernel(x_ref, w1_ref, w2_ref, lg_ref, lb_ref, o_ref, acc_scr, *, alpha):
    k = pl.program_id(1)

    @pl.when(k == 0)
    def _():
        acc_scr[...] = jnp.zeros(acc_scr.shape, F32)

    h = jnp.maximum(_dot(x_ref[...].astype(BF16), w1_ref[...]), 0.0)
    acc_scr[...] += _dot((h * h).astype(BF16), w2_ref[...])

    @pl.when(k == pl.num_programs(1) - 1)
    def _():
        o_ref[...] = _layer_norm(alpha * x_ref[...] + acc_scr[...], lg_ref[...], lb_ref[...])


def _ffn(x, w1, w2, ln_g, ln_b, alpha, tm=1024, tf=1024):
    m, d = x.shape
    tm = min(tm, m)
    dff = w1.shape[1]
    return pl.pallas_call(
        functools.partial(_ffn_kernel, alpha=alpha),
        out_shape=jax.ShapeDtypeStruct((m, d), F32),
        grid=(m // tm, dff // tf),
        in_specs=[pl.BlockSpec((tm, d), lambda i, k: (i, 0)),
                  pl.BlockSpec((d, tf), lambda i, k: (0, k)),
                  pl.BlockSpec((tf, d), lambda i, k: (k, 0)),
                  pl.BlockSpec((1, d), lambda i, k: (0, 0)),
                  pl.BlockSpec((1, d), lambda i, k: (0, 0))],
        out_specs=pl.BlockSpec((tm, d), lambda i, k: (i, 0)),
        scratch_shapes=[pltpu.VMEM((tm, d), F32)],
        compiler_params=_cparams(("parallel", "arbitrary")),
        name="ffn",
    )(x, w1.astype(BF16), w2.astype(BF16), ln_g.reshape(1, d), ln_b.reshape(1, d))


def kernel(x, w_in, b_gate, a_lambda, a_subln_g, b_conv_w, b_a_log, b_dt_bias, b_norm_g,
           c_q_norm_g, c_kv_norm_g, c_kidx_g, c_kidx_b, c_w_uq, c_w_qidx, c_w_uk, c_w_uv,
           w_branch_a, w_branch_b, w_branch_c, w_o, ln1_g, ln1_b, w_ff1, w_ff2, ln2_g, ln2_b):
    bsz, t, d = x.shape
    depth = w_in.shape[0]
    alpha = (2 * depth) ** 0.25
    xf = x.reshape(bsz * t, d)
    for l in range(depth):
        lam_init = 0.8 - 0.6 * math.exp(-0.3 * l)
        w_main, w_kv = _pack_w_in(w_in[l])
        proj = _matmul(xf, w_main, F32, tm=min(1024, bsz * t), tn=1024)
        proj_kv = _matmul(xf, w_kv, BF16, tm=min(1024, bsz * t), tn=1024)
        y_a = _diff_attention(proj, proj_kv, a_lambda[l], a_subln_g[l], bsz, t, lam_init)
        y_b = _gated_deltanet(proj, b_conv_w[l], b_a_log[l], b_dt_bias[l], b_norm_g[l], bsz, t)
        y_c = _dsa_attention(proj, c_q_norm_g[l], c_kv_norm_g[l], c_kidx_g[l], c_kidx_b[l],
                             c_w_uq[l], c_w_qidx[l], c_w_uk[l], c_w_uv[l], bsz, t)
        xf = _merge(xf, proj, y_a, y_b, y_c, b_gate[l], w_branch_a[l], w_branch_b[l], w_branch_c[l],
                    w_o[l], ln1_g[l], ln1_b[l], alpha)
        xf = _ffn(xf, w_ff1[l], w_ff2[l], ln2_g[l], ln2_b[l], alpha)
    return xf.reshape(bsz, t, d)
```
